```python
import jax, jax.numpy as jnp
from jax import lax
import numpy as np

D_MODEL = 1024
BATCH = 8
SEQ = 2048
DEPTH = 4

CHUNK = 64
D_POOL = D_MODEL // 2
POOL_WINDOWS = (2, 4, 8, 16)
N_POOL_GROUPS = len(POOL_WINDOWS)
POOL_GROUP_DIM = D_POOL // N_POOL_GROUPS
D_ATTN = D_MODEL - D_POOL
N_HEADS = 8
HEAD_DIM = D_ATTN // N_HEADS
LEFT_CHUNKS = 8
BAND = LEFT_CHUNKS + 1
REL_MAX = 128
REL_MIN = CHUNK - 1
N_REL = REL_MIN + REL_MAX + 1
D_FF = 4 * D_MODEL
D_IN = D_POOL + 3 * D_ATTN
N_MOD = 6
ALPHA = (2.0 * DEPTH) ** 0.25
BETA = (8.0 * DEPTH) ** -0.25
LN_EPS = 1e-5
MASK_VALUE = -1e30

kernel_name = "hybrid_pool_chunkattn_deepnorm_adaln"


def layer_norm(x, g, b):
    x32 = x.astype(jnp.float32)
    mu = jnp.mean(x32, axis=-1, keepdims=True)
    var = jnp.mean(jnp.square(x32 - mu), axis=-1, keepdims=True)
    y = (x32 - mu) * lax.rsqrt(var + LN_EPS)
    return (y * g.astype(jnp.float32) + b.astype(jnp.float32)).astype(x.dtype)


def pool_mixer(u, w_pool, pool_scale):
    S = u.shape[1]
    u32 = u.astype(jnp.float32)
    cs = jnp.pad(jnp.cumsum(u32, axis=1), ((0, 0), (1, 0), (0, 0)))
    t = jnp.arange(S)
    outs = []
    for g, w in enumerate(POOL_WINDOWS):
        lo, hi = g * POOL_GROUP_DIM, (g + 1) * POOL_GROUP_DIM
        start = jnp.maximum(t + 1 - w, 0)
        count = (t + 1 - start).astype(jnp.float32)
        window_sum = cs[:, 1:, lo:hi] - cs[:, start, lo:hi]
        pooled = window_sum / count[None, :, None] - u32[..., lo:hi]
        outs.append(jnp.einsum('bsc,cd->bsd', pooled.astype(u.dtype), w_pool[g]))
    return jnp.concatenate(outs, axis=-1) * pool_scale


def chunked_attention(q, k, v, rel_bias):
    B, S, H, Dh = q.shape
    NC = S // CHUNK
    qc = q.reshape(B, NC, CHUNK, H, Dh)
    pad = ((0, 0), (LEFT_CHUNKS, 0), (0, 0), (0, 0), (0, 0))
    kc = jnp.pad(k.reshape(B, NC, CHUNK, H, Dh), pad)
    vc = jnp.pad(v.reshape(B, NC, CHUNK, H, Dh), pad)
    band_idx = jnp.arange(NC)[:, None] + jnp.arange(BAND)[None, :]
    kb = kc[:, band_idx].reshape(B, NC, BAND * CHUNK, H, Dh)
    vb = vc[:, band_idx].reshape(B, NC, BAND * CHUNK, H, Dh)
    scores = jnp.einsum('bnqhd,bnkhd->bnhqk', qc, kb).astype(jnp.float32) * (Dh ** -0.5)
    q_pos = jnp.arange(CHUNK) + LEFT_CHUNKS * CHUNK
    k_pos = jnp.arange(BAND * CHUNK)
    rel = jnp.clip(q_pos[:, None] - k_pos[None, :], -REL_MIN, REL_MAX) + REL_MIN
    bias = rel_bias.astype(jnp.float32)[:, rel]
    valid = jnp.repeat(band_idx >= LEFT_CHUNKS, CHUNK, axis=1)
    scores = jnp.where(valid[None, :, None, None, :], scores + bias[None, None], MASK_VALUE)
    probs = jax.nn.softmax(scores, axis=-1).astype(v.dtype)
    out = jnp.einsum('bnhqk,bnkhd->bnqhd', probs, vb)
    return out.reshape(B, S, H * Dh)


def setup_inputs(seed: int = 0) -> dict:
    key = jax.random.key(seed)
    ks = jax.random.split(key, 16)
    f32 = jnp.float32
    nrm = lambda k, shape, s: jax.random.normal(k, shape, f32) * s
    return {
        "x": nrm(ks[0], (BATCH, SEQ, D_MODEL), 1.0),
        "c": nrm(ks[1], (BATCH, D_MODEL), 1.0),
        "w_ada": nrm(ks[2], (DEPTH, D_MODEL, N_MOD * D_MODEL), 0.1 * D_MODEL ** -0.5),
        "b_ada": nrm(ks[3], (DEPTH, N_MOD * D_MODEL), 0.02),
        "w_in": nrm(ks[4], (DEPTH, D_MODEL, D_IN), D_MODEL ** -0.5),
        "w_pool": nrm(ks[5], (DEPTH, N_POOL_GROUPS, POOL_GROUP_DIM, POOL_GROUP_DIM), POOL_GROUP_DIM ** -0.5),
        "pool_scale": 1.0 + nrm(ks[6], (DEPTH, D_POOL), 0.1),
        "rel_bias": nrm(ks[7], (DEPTH, N_HEADS, N_REL), 0.1),
        "w_out": nrm(ks[8], (DEPTH, D_MODEL, D_MODEL), BETA * D_MODEL ** -0.5),
        "ln1_g": 1.0 + nrm(ks[9], (DEPTH, D_MODEL), 0.02),
        "ln1_b": nrm(ks[10], (DEPTH, D_MODEL), 0.02),
        "w_ff1": nrm(ks[11], (DEPTH, D_MODEL, D_FF), D_MODEL ** -0.5),
        "w_ff2": nrm(ks[12], (DEPTH, D_FF, D_MODEL), BETA * D_FF ** -0.5),
        "ln2_g": 1.0 + nrm(ks[13], (DEPTH, D_MODEL), 0.02),
        "ln2_b": nrm(ks[14], (DEPTH, D_MODEL), 0.02),
    }


def reference(x, c, w_ada, b_ada, w_in, w_pool, pool_scale, rel_bias, w_out,
              ln1_g, ln1_b, w_ff1, w_ff2, ln2_g, ln2_b):
    B, S, _ = x.shape
    c_act = jax.nn.silu(c)
    for l in range(DEPTH):
        mod = jnp.einsum('bd,de->be', c_act, w_ada[l]) + b_ada[l]
        sh1, sc1, g1, sh2, sc2, g2 = [m[:, None, :] for m in jnp.split(mod, N_MOD, axis=-1)]
        h = x * (1.0 + sc1) + sh1
        proj = jnp.einsum('bsd,de->bse', h, w_in[l])
        u = proj[..., :D_POOL]
        q, k, v = jnp.split(proj[..., D_POOL:], 3, axis=-1)
        q = q.reshape(B, S, N_HEADS, HEAD_DIM)
        k = k.reshape(B, S, N_HEADS, HEAD_DIM)
        v = v.reshape(B, S, N_HEADS, HEAD_DIM)
        y_pool = pool_mixer(u, w_pool[l], pool_scale[l])
        y_attn = chunked_attention(q, k, v, rel_bias[l])
        y = jnp.einsum('bse,ed->bsd', jnp.concatenate([y_pool, y_attn], axis=-1), w_out[l])
        x = layer_norm(ALPHA * x + (1.0 + g1) * y, ln1_g[l], ln1_b[l])
        h = x * (1.0 + sc2) + sh2
        f = jnp.square(jax.nn.relu(jnp.einsum('bsd,df->bsf', h, w_ff1[l])))
        f = jnp.einsum('bsf,fd->bsd', f, w_ff2[l])
        x = layer_norm(ALPHA * x + (1.0 + g2) * f, ln2_g[l], ln2_b[l])
    return x
```

```python
import functools

import jax
import jax.numpy as jnp
from jax import lax
from jax.experimental import pallas as pl
from jax.experimental.pallas import tpu as pltpu

D_MODEL = 1024
CHUNK = 64
D_POOL = D_MODEL // 2
POOL_WINDOWS = (2, 4, 8, 16)
POOL_GROUP_DIM = D_POOL // len(POOL_WINDOWS)
D_ATTN = D_MODEL - D_POOL
N_HEADS = 8
HEAD_DIM = D_ATTN // N_HEADS
LEFT_CHUNKS = 8
BAND = LEFT_CHUNKS + 1
REL_MAX = 128
REL_MIN = CHUNK - 1
D_FF = 4 * D_MODEL
D_IN = D_POOL + 3 * D_ATTN
N_MOD = 6
LN_EPS = 1e-5
MASK_VALUE = -1e30

BAND_LEN = BAND * CHUNK
HEADS_PER_GROUP = 2
N_HEAD_GROUPS = N_HEADS // HEADS_PER_GROUP
GROUP_W = HEADS_PER_GROUP * HEAD_DIM
MAX_WINDOW = max(POOL_WINDOWS)

SEQ_TILE = LEFT_CHUNKS * CHUNK
ROW_TILE = 512
ADA_COLS = 1536
FF_COLS = 1024

VMEM_LIMIT = 56 * 1024 * 1024

f32 = jnp.float32
bf16 = jnp.bfloat16


def _resident(shape):
    return pl.BlockSpec(shape, lambda *_: (0,) * len(shape), pipeline_mode=pl.Buffered(1))


def _layer_norm(z, g, b):
    mu = jnp.mean(z, axis=-1, keepdims=True)
    zc = z - mu
    var = jnp.mean(zc * zc, axis=-1, keepdims=True)
    return zc * lax.rsqrt(var + LN_EPS) * g + b


def _ada_kernel(c_ref, w_ref, b_ref, o_ref):
    c = c_ref[...]
    c_act = c / (1.0 + jnp.exp(-c))
    o_ref[0] = jnp.dot(c_act, w_ref[0], preferred_element_type=f32) + b_ref[0]


def _ada_call(c, w_ada, b_ada):
    depth, d, n = w_ada.shape
    batch = c.shape[0]
    return pl.pallas_call(
        _ada_kernel,
        grid=(depth, n // ADA_COLS),
        in_specs=[
            pl.BlockSpec((batch, d), lambda l, j: (0, 0)),
            pl.BlockSpec((1, d, ADA_COLS), lambda l, j: (l, 0, j)),
            pl.BlockSpec((1, 1, ADA_COLS), lambda l, j: (l, 0, j)),
        ],
        out_specs=pl.BlockSpec((1, batch, ADA_COLS), lambda l, j: (l, 0, j)),
        out_shape=jax.ShapeDtypeStruct((depth, batch, n), f32),
        compiler_params=pltpu.CompilerParams(
            dimension_semantics=("arbitrary", "arbitrary"), vmem_limit_bytes=VMEM_LIMIT),
        name="ada_mod",
    )(c, w_ada, b_ada.reshape(depth, 1, n))


def _in_proj_kernel(x_ref, mod_ref, w_ref, u_ref, qkv_ref):
    x = x_ref[0]
    sh1 = mod_ref[0, 0:1, :]
    sc1 = mod_ref[0, 1:2, :]
    h = (x * (1.0 + sc1) + sh1).astype(bf16)
    u_ref[0] = jnp.dot(h, w_ref[:, 0:D_POOL], preferred_element_type=f32)
    q = jnp.dot(h, w_ref[:, D_POOL:D_POOL + D_ATTN], preferred_element_type=f32)
    qkv_ref[0, :, 0:D_ATTN] = (q * (HEAD_DIM ** -0.5)).astype(bf16)
    for n in (1, 2):
        lo = D_POOL + n * D_ATTN
        kv = jnp.dot(h, w_ref[:, lo:lo + D_ATTN], preferred_element_type=f32)
        qkv_ref[0, :, n * D_ATTN:(n + 1) * D_ATTN] = kv.astype(bf16)


def _in_proj_call(x, mod_l, w_in_l):
    batch, seq, d = x.shape
    return pl.pallas_call(
        _in_proj_kernel,
        grid=(batch, seq // ROW_TILE),
        in_specs=[
            pl.BlockSpec((1, ROW_TILE, d), lambda b, i: (b, i, 0)),
            pl.BlockSpec((1, N_MOD, d), lambda b, i: (b, 0, 0)),
            _resident((d, D_IN)),
        ],
        out_specs=[
            pl.BlockSpec((1, ROW_TILE, D_POOL), lambda b, i: (b, i, 0)),
            pl.BlockSpec((1, ROW_TILE, 3 * D_ATTN), lambda b, i: (b, i, 0)),
        ],
        out_shape=[
            jax.ShapeDtypeStruct((batch, seq, D_POOL), f32),
            jax.ShapeDtypeStruct((batch, seq, 3 * D_ATTN), bf16),
        ],
        compiler_params=pltpu.CompilerParams(
            dimension_semantics=("arbitrary", "arbitrary"), vmem_limit_bytes=VMEM_LIMIT),
        name="in_proj",
    )(x, mod_l, w_in_l)


def _mixer_kernel(u_ref, qkv_ref, bias_ref, wp_ref, ps_ref, y_ref, kv_scr, u_scr):
    j = pl.program_id(1)

    @pl.when(j == 0)
    def _():
        kv_scr[0:SEQ_TILE, :] = jnp.zeros((SEQ_TILE, 2 * D_ATTN), bf16)
        u_scr[0:MAX_WINDOW, :] = jnp.zeros((MAX_WINDOW, D_POOL), f32)

    kv_scr[SEQ_TILE:2 * SEQ_TILE, :] = qkv_ref[0, :, D_ATTN:3 * D_ATTN]
    u_scr[MAX_WINDOW:MAX_WINDOW + SEQ_TILE, :] = u_ref[0]

    t_glob = j * SEQ_TILE + lax.broadcasted_iota(jnp.int32, (SEQ_TILE, 1), 0)
    for g, w in enumerate(POOL_WINDOWS):
        cols = slice(g * POOL_GROUP_DIM, (g + 1) * POOL_GROUP_DIM)
        tok = u_scr[MAX_WINDOW:MAX_WINDOW + SEQ_TILE, cols]
        win = tok
        for i in range(1, w):
            win = win + u_scr[MAX_WINDOW - i:MAX_WINDOW - i + SEQ_TILE, cols]
        count = jnp.minimum(t_glob + 1, w).astype(f32)
        pooled = win / count - tok
        yp = jnp.dot(pooled.astype(bf16), wp_ref[g], preferred_element_type=f32)
        y_ref[0, :, cols] = (yp * ps_ref[:, cols]).astype(bf16)

    lane = lax.broadcasted_iota(jnp.int32, (CHUNK, GROUP_W), 1)
    first_head = lane < HEAD_DIM
    col = lax.broadcasted_iota(jnp.int32, (HEADS_PER_GROUP * CHUNK, BAND_LEN), 1)

    def chunk_body(c, carry):
        r0 = pl.multiple_of(c * CHUNK, CHUNK)
        first_valid = jnp.where(j == 0, SEQ_TILE - c * CHUNK, 0)
        valid = col >= first_valid
        for p in range(N_HEAD_GROUPS):
            lo = p * GROUP_W
            qp = qkv_ref[0, pl.ds(r0, CHUNK), lo:lo + GROUP_W]
            zero = jnp.zeros_like(qp)
            qs = jnp.concatenate([jnp.where(first_head, qp, zero),
                                  jnp.where(first_head, zero, qp)], axis=0)
            kp = kv_scr[pl.ds(r0, BAND_LEN), lo:lo + GROUP_W]
            vp = kv_scr[pl.ds(r0, BAND_LEN), D_ATTN + lo:D_ATTN + lo + GROUP_W]
            s = lax.dot_general(qs, kp, (((1,), (1,)), ((), ())), preferred_element_type=f32)
            s = jnp.where(valid, s + bias_ref[p], MASK_VALUE)
            m = jnp.max(s, axis=-1, keepdims=True)
            e = jnp.exp(s - m)
            denom = jnp.sum(e, axis=-1, keepdims=True)
            o = jnp.dot(e.astype(bf16), vp, preferred_element_type=f32) / denom
            out = jnp.where(first_head, o[0:CHUNK], o[CHUNK:2 * CHUNK])
            y_ref[0, pl.ds(r0, CHUNK), D_POOL + lo:D_POOL + lo + GROUP_W] = out.astype(bf16)
        return carry

    lax.fori_loop(0, SEQ_TILE // CHUNK, chunk_body, 0)

    kv_scr[0:SEQ_TILE, :] = kv_scr[SEQ_TILE:2 * SEQ_TILE, :]
    u_scr[0:MAX_WINDOW, :] = u_scr[SEQ_TILE:SEQ_TILE + MAX_WINDOW, :]


def _mixer_call(u, qkv, bias_l, w_pool_l, pool_scale_l):
    batch, seq, _ = u.shape
    return pl.pallas_call(
        _mixer_kernel,
        grid=(batch, seq // SEQ_TILE),
        in_specs=[
            pl.BlockSpec((1, SEQ_TILE, D_POOL), lambda b, j: (b, j, 0)),
            pl.BlockSpec((1, SEQ_TILE, 3 * D_ATTN), lambda b, j: (b, j, 0)),
            _resident(bias_l.shape),
            _resident(w_pool_l.shape),
            _resident(pool_scale_l.shape),
        ],
        out_specs=pl.BlockSpec((1, SEQ_TILE, D_MODEL), lambda b, j: (b, j, 0)),
        out_shape=jax.ShapeDtypeStruct((batch, seq, D_MODEL), bf16),
        scratch_shapes=[
            pltpu.VMEM((2 * SEQ_TILE, 2 * D_ATTN), bf16),
            pltpu.VMEM((MAX_WINDOW + SEQ_TILE, D_POOL), f32),
        ],
        compiler_params=pltpu.CompilerParams(
            dimension_semantics=("arbitrary", "arbitrary"), vmem_limit_bytes=VMEM_LIMIT),
        name="mixer",
    )(u, qkv, bias_l, w_pool_l, pool_scale_l)


def _out_ffn_kernel(alpha, y_ref, x_ref, mod_ref, wo_ref, ln1_ref, w1_ref, w2_ref, ln2_ref, o_ref):
    x = x_ref[0]
    g1 = mod_ref[0, 2:3, :]
    sh2 = mod_ref[0, 3:4, :]
    sc2 = mod_ref[0, 4:5, :]
    g2 = mod_ref[0, 5:6, :]
    a = jnp.dot(y_ref[0], wo_ref[...], preferred_element_type=f32)
    x1 = _layer_norm(alpha * x + (1.0 + g1) * a, ln1_ref[0:1, :], ln1_ref[1:2, :])
    h = (x1 * (1.0 + sc2) + sh2).astype(bf16)
    acc = None
    for s in range(D_FF // FF_COLS):
        cols = slice(s * FF_COLS, (s + 1) * FF_COLS)
        f = jnp.dot(h, w1_ref[:, cols], preferred_element_type=f32)
        f = jnp.square(jnp.maximum(f, 0.0)).astype(bf16)
        part = jnp.dot(f, w2_ref[cols, :], preferred_element_type=f32)
        acc = part if acc is None else acc + part
    o_ref[0] = _layer_norm(alpha * x1 + (1.0 + g2) * acc, ln2_ref[0:1, :], ln2_ref[1:2, :])


def _out_ffn_call(alpha, y, x, mod_l, w_out_l, ln1_l, w_ff1_l, w_ff2_l, ln2_l):
    batch, seq, d = x.shape
    tile = lambda b, i: (b, i, 0)
    return pl.pallas_call(
        functools.partial(_out_ffn_kernel, alpha),
        grid=(batch, seq // ROW_TILE),
        in_specs=[
            pl.BlockSpec((1, ROW_TILE, d), tile),
            pl.BlockSpec((1, ROW_TILE, d), tile),
            pl.BlockSpec((1, N_MOD, d), lambda b, i: (b, 0, 0)),
            _resident((d, d)),
            _resident((2, d)),
            _resident((d, D_FF)),
            _resident((D_FF, d)),
            _resident((2, d)),
        ],
        out_specs=pl.BlockSpec((1, ROW_TILE, d), tile),
        out_shape=jax.ShapeDtypeStruct((batch, seq, d), f32),
        compiler_params=pltpu.CompilerParams(
            dimension_semantics=("arbitrary", "arbitrary"), vmem_limit_bytes=VMEM_LIMIT),
        name="out_ffn",
    )(y, x, mod_l, w_out_l, ln1_l, w_ff1_l, w_ff2_l, ln2_l)


def _band_bias(rel_bias_l):
    q_pos = jnp.arange(CHUNK) + LEFT_CHUNKS * CHUNK
    k_pos = jnp.arange(BAND_LEN)
    rel = jnp.clip(q_pos[:, None] - k_pos[None, :], -REL_MIN, REL_MAX) + REL_MIN
    bias = rel_bias_l.astype(f32)[:, rel]
    return bias.reshape(N_HEAD_GROUPS, HEADS_PER_GROUP * CHUNK, BAND_LEN)


def kernel(x, c, w_ada, b_ada, w_in, w_pool, pool_scale, rel_bias, w_out, ln1_g, ln1_b,
           w_ff1, w_ff2, ln2_g, ln2_b):
    depth = w_in.shape[0]
    batch = x.shape[0]
    alpha = (2.0 * depth) ** 0.25
    mod = _ada_call(c, w_ada, b_ada).reshape(depth, batch, N_MOD, D_MODEL)
    for l in range(depth):
        u, qkv = _in_proj_call(x, mod[l], w_in[l].astype(bf16))
        y = _mixer_call(u, qkv, _band_bias(rel_bias[l]), w_pool[l].astype(bf16),
                        pool_scale[l].reshape(1, D_POOL))
        x = _out_ffn_call(alpha, y, x, mod[l], w_out[l].astype(bf16),
                          jnp.stack([ln1_g[l], ln1_b[l]]), w_ff1[l].astype(bf16),
                          w_ff2[l].astype(bf16), jnp.stack([ln2_g[l], ln2_b[l]]))
    return x
```

```python
import functools
import math

import jax
import jax.numpy as jnp
from jax import lax
from jax.experimental import pallas as pl
from jax.experimental.pallas import tpu as pltpu

D_MODEL = 1024
CHUNK = 64
D_POOL = D_MODEL // 2
POOL_WINDOWS = (2, 4, 8, 16)
POOL_GROUP_DIM = D_POOL // len(POOL_WINDOWS)
D_ATTN = D_MODEL - D_POOL
N_HEADS = 8
HEAD_DIM = D_ATTN // N_HEADS
LEFT_CHUNKS = 8
BAND = LEFT_CHUNKS + 1
REL_MAX = 128
REL_MIN = CHUNK - 1
N_REL = REL_MIN + REL_MAX + 1
D_FF = 4 * D_MODEL
D_IN = D_POOL + 3 * D_ATTN
N_MOD = 6
LN_EPS = 1e-5
MASK_VALUE = -1e30
LOG2E = math.log2(math.e)

LANES = 128
BAND_LEN = BAND * CHUNK
BAND_PAD = BAND_LEN + CHUNK
HEADS_PER_GROUP = LANES // HEAD_DIM
N_HEAD_GROUPS = N_HEADS // HEADS_PER_GROUP
GROUP_ROWS = HEADS_PER_GROUP * CHUNK
MAX_WINDOW = max(POOL_WINDOWS)

SEQ_TILE = LEFT_CHUNKS * CHUNK
CHUNKS_PER_TILE = SEQ_TILE // CHUNK
ROW_TILE = 512
ADA_COLS = 1536
FF_COLS = 1024

VMEM_LIMIT = 56 * 1024 * 1024

f32 = jnp.float32
bf16 = jnp.bfloat16


def _layer_block(l, shape):
    return pl.BlockSpec((1, *shape), lambda *_: (l,) + (0,) * len(shape),
                        pipeline_mode=pl.Buffered(1))


def _params():
    return pltpu.CompilerParams(dimension_semantics=("arbitrary", "arbitrary"),
                                vmem_limit_bytes=VMEM_LIMIT)


def _layer_norm(z, g, b):
    mu = jnp.mean(z, axis=-1, keepdims=True)
    zc = z - mu
    var = jnp.mean(zc * zc, axis=-1, keepdims=True)
    return zc * lax.rsqrt(var + LN_EPS) * g + b


def _ada_kernel(c_ref, w_ref, b_ref, o_ref):
    c = c_ref[...]
    c_act = c / (1.0 + jnp.exp(-c))
    o_ref[0] = jnp.dot(c_act, w_ref[0], preferred_element_type=f32) + b_ref[0]


def _ada_call(c, w_ada, b_ada):
    depth, d, n = w_ada.shape
    batch = c.shape[0]
    return pl.pallas_call(
        _ada_kernel,
        grid=(depth, n // ADA_COLS),
        in_specs=[
            pl.BlockSpec((batch, d), lambda l, j: (0, 0)),
            pl.BlockSpec((1, d, ADA_COLS), lambda l, j: (l, 0, j)),
            pl.BlockSpec((1, 1, ADA_COLS), lambda l, j: (l, 0, j)),
        ],
        out_specs=pl.BlockSpec((1, batch, ADA_COLS), lambda l, j: (l, 0, j)),
        out_shape=jax.ShapeDtypeStruct((depth, batch, n), f32),
        compiler_params=_params(),
        name="ada_mod",
    )(c, w_ada, b_ada.reshape(depth, 1, n))


def _in_proj_kernel(x_ref, mod_ref, w_ref, u_ref, qkv_ref):
    x = x_ref[0]
    sh1 = mod_ref[0, 0, 0:1, :]
    sc1 = mod_ref[0, 0, 1:2, :]
    h = (x * (1.0 + sc1) + sh1).astype(bf16)
    u_ref[0] = jnp.dot(h, w_ref[0, :, 0:D_POOL], preferred_element_type=f32)
    q = jnp.dot(h, w_ref[0, :, D_POOL:D_POOL + D_ATTN], preferred_element_type=f32)
    qkv_ref[0, :, 0:D_ATTN] = (q * (HEAD_DIM ** -0.5 * LOG2E)).astype(bf16)
    for n in (1, 2):
        lo = D_POOL + n * D_ATTN
        kv = jnp.dot(h, w_ref[0, :, lo:lo + D_ATTN], preferred_element_type=f32)
        qkv_ref[0, :, n * D_ATTN:(n + 1) * D_ATTN] = kv.astype(bf16)


def _in_proj_call(l, x, mod, w_in):
    batch, seq, d = x.shape
    return pl.pallas_call(
        _in_proj_kernel,
        grid=(batch, seq // ROW_TILE),
        in_specs=[
            pl.BlockSpec((1, ROW_TILE, d), lambda b, i: (b, i, 0)),
            pl.BlockSpec((1, 1, N_MOD, d), lambda b, i: (l, b, 0, 0)),
            _layer_block(l, (d, D_IN)),
        ],
        out_specs=[
            pl.BlockSpec((1, ROW_TILE, D_POOL), lambda b, i: (b, i, 0)),
            pl.BlockSpec((1, ROW_TILE, 3 * D_ATTN), lambda b, i: (b, i, 0)),
        ],
        out_shape=[
            jax.ShapeDtypeStruct((batch, seq, D_POOL), f32),
            jax.ShapeDtypeStruct((batch, seq, 3 * D_ATTN), bf16),
        ],
        compiler_params=_params(),
        name="in_proj",
    )(x, mod, w_in)


def _pool_tile(j, u_ref, halo_ref, wp_ref, ps_ref, y_ref):
    halo = jnp.where(j > 0, halo_ref[0], 0.0)
    t_head = j * SEQ_TILE + lax.broadcasted_iota(jnp.int32, (MAX_WINDOW, POOL_GROUP_DIM), 0)
    for g, w in enumerate(POOL_WINDOWS):
        cols = slice(g * POOL_GROUP_DIM, (g + 1) * POOL_GROUP_DIM)
        tok = u_ref[0, :, cols]
        win = jnp.concatenate([halo[:, cols], tok], axis=0)
        span = 1
        while span < w:
            win = win + pltpu.roll(win, span, axis=0)
            span *= 2
        win = win[MAX_WINDOW:]
        head = win[:MAX_WINDOW] / jnp.minimum(t_head + 1, w).astype(f32) - tok[:MAX_WINDOW]
        rest = win[MAX_WINDOW:] * (1.0 / w) - tok[MAX_WINDOW:]
        pooled = jnp.concatenate([head, rest], axis=0)
        yp = jnp.dot(pooled.astype(bf16), wp_ref[0, g], preferred_element_type=f32)
        y_ref[0, :, cols] = (yp * ps_ref[0, :, cols]).astype(bf16)


def _attn_tile(first_tile, q_ref, k_ref, v_ref, pk_ref, pv_ref, bias_ref, y_ref, s_scr, m_scr):
    lane = lax.broadcasted_iota(jnp.int32, (CHUNK, LANES), 1)
    first_head = lane < HEAD_DIM
    nt = (((1,), (1,)), ((), ()))

    def extents(c):
        a0 = LANES * (c // 2)
        b1 = LANES * (c // 2 + 1)
        w_prev = 0 if first_tile else SEQ_TILE - a0
        return a0, b1, w_prev

    def scores(c):
        a0, b1, w_prev = extents(c)
        b_lo = SEQ_TILE - a0 - w_prev
        for p in range(N_HEAD_GROUPS):
            grp = slice(p * LANES, (p + 1) * LANES)
            qp = q_ref[0, c * CHUNK:(c + 1) * CHUNK, grp]
            zero = jnp.zeros_like(qp)
            qs = jnp.concatenate([jnp.where(first_head, qp, zero),
                                  jnp.where(first_head, zero, qp)], axis=0)
            s = lax.dot_general(qs, k_ref[0, 0:b1, grp], nt, preferred_element_type=f32)
            if w_prev:
                s = jnp.concatenate(
                    [lax.dot_general(qs, pk_ref[0, a0:SEQ_TILE, grp], nt,
                                     preferred_element_type=f32), s], axis=1)
            s = s + bias_ref[0, c % 2, p, :, b_lo:b_lo + w_prev + b1]
            s_scr[c % 2, p, :, 0:w_prev + b1] = s
            m_scr[c % 2, p] = jnp.broadcast_to(jnp.max(s, axis=-1, keepdims=True),
                                               (GROUP_ROWS, LANES))

    def normalise(c):
        a0, b1, w_prev = extents(c)
        width = w_prev + b1
        for p in range(N_HEAD_GROUPS):
            grp = slice(p * LANES, (p + 1) * LANES)
            m = m_scr[c % 2, p]
            e = jnp.concatenate(
                [jnp.exp2(s_scr[c % 2, p, :, lo:lo + LANES] - m) for lo in range(0, width, LANES)],
                axis=1)
            denom = jnp.sum(e, axis=-1, keepdims=True)
            eb = e.astype(bf16)
            o = jnp.dot(eb[:, w_prev:], v_ref[0, 0:b1, grp], preferred_element_type=f32)
            if w_prev:
                o = o + jnp.dot(eb[:, :w_prev], pv_ref[0, a0:SEQ_TILE, grp],
                                preferred_element_type=f32)
            o = o / denom
            out = jnp.where(first_head, o[0:CHUNK], o[CHUNK:GROUP_ROWS])
            y_ref[0, c * CHUNK:(c + 1) * CHUNK, D_POOL + p * LANES:D_POOL + (p + 1) * LANES] = (
                out.astype(bf16))

    scores(0)
    for c in range(CHUNKS_PER_TILE):
        if c + 1 < CHUNKS_PER_TILE:
            scores(c + 1)
        normalise(c)


def _mixer_kernel(u_ref, halo_ref, q_ref, k_ref, v_ref, pk_ref, pv_ref, bias_ref, wp_ref, ps_ref,
                  y_ref, s_scr, m_scr):
    j = pl.program_id(1)
    _pool_tile(j, u_ref, halo_ref, wp_ref, ps_ref, y_ref)
    attn = functools.partial(_attn_tile, q_ref=q_ref, k_ref=k_ref, v_ref=v_ref, pk_ref=pk_ref,
                             pv_ref=pv_ref, bias_ref=bias_ref, y_ref=y_ref, s_scr=s_scr,
                             m_scr=m_scr)
    pl.when(j == 0)(functools.partial(attn, True))
    pl.when(j > 0)(functools.partial(attn, False))


def _mixer_call(l, u, qkv, bias, w_pool, pool_scale):
    batch, seq, _ = u.shape
    halo_blocks = SEQ_TILE // MAX_WINDOW
    cur = lambda part: pl.BlockSpec((1, SEQ_TILE, D_ATTN), lambda b, j: (b, j, part))
    prev = lambda part: pl.BlockSpec((1, SEQ_TILE, D_ATTN),
                                     lambda b, j: (b, jnp.maximum(j - 1, 0), part))
    return pl.pallas_call(
        _mixer_kernel,
        grid=(batch, seq // SEQ_TILE),
        in_specs=[
            pl.BlockSpec((1, SEQ_TILE, D_POOL), lambda b, j: (b, j, 0)),
            pl.BlockSpec((1, MAX_WINDOW, D_POOL),
                         lambda b, j: (b, jnp.maximum(j * halo_blocks - 1, 0), 0)),
            cur(0), cur(1), cur(2), prev(1), prev(2),
            _layer_block(l, bias.shape[1:]),
            _layer_block(l, w_pool.shape[1:]),
            _layer_block(l, pool_scale.shape[1:]),
        ],
        out_specs=pl.BlockSpec((1, SEQ_TILE, D_MODEL), lambda b, j: (b, j, 0)),
        out_shape=jax.ShapeDtypeStruct((batch, seq, D_MODEL), bf16),
        scratch_shapes=[pltpu.VMEM((2, N_HEAD_GROUPS, GROUP_ROWS, BAND_PAD), f32),
                        pltpu.VMEM((2, N_HEAD_GROUPS, GROUP_ROWS, LANES), f32)],
        compiler_params=_params(),
        name="mixer",
    )(u, u, qkv, qkv, qkv, qkv, qkv, bias, w_pool, pool_scale)


def _out_ffn_kernel(alpha, y_ref, x_ref, mod_ref, wo_ref, ln1_ref, w1_ref, w2_ref, ln2_ref, o_ref):
    x = x_ref[0]
    g1 = mod_ref[0, 0, 2:3, :]
    sh2 = mod_ref[0, 0, 3:4, :]
    sc2 = mod_ref[0, 0, 4:5, :]
    g2 = mod_ref[0, 0, 5:6, :]
    a = jnp.dot(y_ref[0], wo_ref[0], preferred_element_type=f32)
    x1 = _layer_norm(alpha * x + (1.0 + g1) * a, ln1_ref[0, 0:1, :], ln1_ref[0, 1:2, :])
    h = (x1 * (1.0 + sc2) + sh2).astype(bf16)
    acc = None
    for s in range(D_FF // FF_COLS):
        cols = slice(s * FF_COLS, (s + 1) * FF_COLS)
        f = jnp.dot(h, w1_ref[0, :, cols], preferred_element_type=f32)
        f = jnp.square(jnp.maximum(f, 0.0)).astype(bf16)
        part = jnp.dot(f, w2_ref[0, cols, :], preferred_element_type=f32)
        acc = part if acc is None else acc + part
    o_ref[0] = _layer_norm(alpha * x1 + (1.0 + g2) * acc, ln2_ref[0, 0:1, :], ln2_ref[0, 1:2, :])


def _out_ffn_call(l, alpha, y, x, mod, w_out, ln1, w_ff1, w_ff2, ln2):
    batch, seq, d = x.shape
    tile = lambda b, i: (b, i, 0)
    return pl.pallas_call(
        functools.partial(_out_ffn_kernel, alpha),
        grid=(batch, seq // ROW_TILE),
        in_specs=[
            pl.BlockSpec((1, ROW_TILE, d), tile),
            pl.BlockSpec((1, ROW_TILE, d), tile),
            pl.BlockSpec((1, 1, N_MOD, d), lambda b, i: (l, b, 0, 0)),
            _layer_block(l, (d, d)),
            _layer_block(l, (2, d)),
            _layer_block(l, (d, D_FF)),
            _layer_block(l, (D_FF, d)),
            _layer_block(l, (2, d)),
        ],
        out_specs=pl.BlockSpec((1, ROW_TILE, d), tile),
        out_shape=jax.ShapeDtypeStruct((batch, seq, d), f32),
        compiler_params=_params(),
        name="out_ffn",
    )(y, x, mod, w_out, ln1, w_ff1, w_ff2, ln2)


def _band_bias_tables(rel_bias):
    tab = rel_bias.astype(f32) * LOG2E
    lead = tab.shape[:-1]
    far = tab[..., -1:]
    n_near = REL_MAX + CHUNK
    period = n_near + CHUNK
    ring = jnp.concatenate([tab[..., ::-1], jnp.broadcast_to(far, (*lead, period - N_REL))], -1)
    near = jnp.tile(ring, (1, 1, CHUNK))[..., :CHUNK * (period - 1)]
    near = near.reshape(*lead, CHUNK, period - 1)[..., :n_near]
    clipped = jnp.broadcast_to(far[..., None], (*lead, CHUNK, BAND_LEN - n_near))
    band = jnp.concatenate([clipped, near], axis=-1)
    masked = jnp.full((*lead, CHUNK, CHUNK), MASK_VALUE, f32)
    tables = jnp.stack([jnp.concatenate([band, masked], -1),
                        jnp.concatenate([masked, band], -1)], axis=1)
    return tables.reshape(lead[0], 2, N_HEAD_GROUPS, GROUP_ROWS, BAND_PAD)


def kernel(x, c, w_ada, b_ada, w_in, w_pool, pool_scale, rel_bias, w_out, ln1_g, ln1_b,
           w_ff1, w_ff2, ln2_g, ln2_b):
    depth = w_in.shape[0]
    batch = x.shape[0]
    alpha = (2.0 * depth) ** 0.25
    mod = _ada_call(c, w_ada, b_ada).reshape(depth, batch, N_MOD, D_MODEL)
    bias = _band_bias_tables(rel_bias)
    w_in, w_pool, w_out, w_ff1, w_ff2 = (w.astype(bf16) for w in (w_in, w_pool, w_out, w_ff1, w_ff2))
    pool_scale = pool_scale.reshape(depth, 1, D_POOL)
    ln1 = jnp.stack([ln1_g, ln1_b], axis=1)
    ln2 = jnp.stack([ln2_g, ln2_b], axis=1)
    for l in range(depth):
        u, qkv = _in_proj_call(l, x, mod, w_in)
        y = _mixer_call(l, u, qkv, bias, w_pool, pool_scale)
        x = _out_ffn_call(l, alpha, y, x, mod, w_out, ln1, w_ff1, w_ff2, ln2)
    return x
```

```python
import functools
import math

import jax
import jax.numpy as jnp
from jax import lax
from jax.experimental import pallas as pl
from jax.experimental.pallas import tpu as pltpu

D_MODEL = 1024
CHUNK = 64
D_POOL = D_MODEL // 2
POOL_WINDOWS = (2, 4, 8, 16)
POOL_GROUP_DIM = D_POOL // len(POOL_WINDOWS)
D_ATTN = D_MODEL - D_POOL
N_HEADS = 8
HEAD_DIM = D_ATTN // N_HEADS
LEFT_CHUNKS = 8
BAND = LEFT_CHUNKS + 1
REL_MAX = 128
REL_MIN = CHUNK - 1
N_REL = REL_MIN + REL_MAX + 1
D_FF = 4 * D_MODEL
D_IN = D_POOL + 3 * D_ATTN
N_MOD = 6
LN_EPS = 1e-5
MASK_VALUE = -1e30
LOG2E = math.log2(math.e)

LANES = 128
BAND_LEN = BAND * CHUNK
BAND_PAD = BAND_LEN + CHUNK
HEADS_PER_GROUP = LANES // HEAD_DIM
N_HEAD_GROUPS = N_HEADS // HEADS_PER_GROUP
GROUP_ROWS = HEADS_PER_GROUP * CHUNK
MAX_WINDOW = max(POOL_WINDOWS)

SEQ_TILE = LEFT_CHUNKS * CHUNK
CHUNKS_PER_TILE = SEQ_TILE // CHUNK
ROW_TILE = 512
FFN_CHAINS = 2
IN_ROWS = 1024
ADA_COLS = 1536
FF_COLS = 1024

VMEM_LIMIT = 56 * 1024 * 1024

f32 = jnp.float32
bf16 = jnp.bfloat16


def _layer_block(l, shape):
    return pl.BlockSpec((1, *shape), lambda *_: (l,) + (0,) * len(shape),
                        pipeline_mode=pl.Buffered(1))


def _params():
    return pltpu.CompilerParams(dimension_semantics=("arbitrary", "arbitrary"),
                                vmem_limit_bytes=VMEM_LIMIT)


def _layer_norm(z, g, b):
    mu = jnp.mean(z, axis=-1, keepdims=True)
    zc = z - mu
    var = jnp.mean(zc * zc, axis=-1, keepdims=True)
    return zc * lax.rsqrt(var + LN_EPS) * g + b


def _ada_kernel(c_ref, w_ref, b_ref, o_ref):
    c = c_ref[...]
    c_act = c / (1.0 + jnp.exp(-c))
    o_ref[0] = jnp.dot(c_act, w_ref[0], preferred_element_type=f32) + b_ref[0]


def _ada_call(c, w_ada, b_ada):
    depth, d, n = w_ada.shape
    batch = c.shape[0]
    return pl.pallas_call(
        _ada_kernel,
        grid=(depth, n // ADA_COLS),
        in_specs=[
            pl.BlockSpec((batch, d), lambda l, j: (0, 0)),
            pl.BlockSpec((1, d, ADA_COLS), lambda l, j: (l, 0, j)),
            pl.BlockSpec((1, 1, ADA_COLS), lambda l, j: (l, 0, j)),
        ],
        out_specs=pl.BlockSpec((1, batch, ADA_COLS), lambda l, j: (l, 0, j)),
        out_shape=jax.ShapeDtypeStruct((depth, batch, n), f32),
        compiler_params=_params(),
        name="ada_mod",
    )(c, w_ada, b_ada.reshape(depth, 1, n))


def _in_proj_kernel(x_ref, mod_ref, w_ref, u_ref, qkv_ref):
    x = x_ref[0]
    sh1 = mod_ref[0, 0, 0:1, :]
    sc1 = mod_ref[0, 0, 1:2, :]
    h = (x * (1.0 + sc1) + sh1).astype(bf16)
    u_ref[0] = jnp.dot(h, w_ref[0, :, 0:D_POOL], preferred_element_type=f32)
    q = jnp.dot(h, w_ref[0, :, D_POOL:D_POOL + D_ATTN], preferred_element_type=f32)
    qkv_ref[0, :, 0:D_ATTN] = (q * (HEAD_DIM ** -0.5 * LOG2E)).astype(bf16)
    for n in (1, 2):
        lo = D_POOL + n * D_ATTN
        kv = jnp.dot(h, w_ref[0, :, lo:lo + D_ATTN], preferred_element_type=f32)
        qkv_ref[0, :, n * D_ATTN:(n + 1) * D_ATTN] = kv.astype(bf16)


def _in_proj_call(l, x, mod, w_in):
    batch, seq, d = x.shape
    return pl.pallas_call(
        _in_proj_kernel,
        grid=(batch, seq // IN_ROWS),
        in_specs=[
            pl.BlockSpec((1, IN_ROWS, d), lambda b, i: (b, i, 0)),
            pl.BlockSpec((1, 1, N_MOD, d), lambda b, i: (l, b, 0, 0)),
            _layer_block(l, (d, D_IN)),
        ],
        out_specs=[
            pl.BlockSpec((1, IN_ROWS, D_POOL), lambda b, i: (b, i, 0)),
            pl.BlockSpec((1, IN_ROWS, 3 * D_ATTN), lambda b, i: (b, i, 0)),
        ],
        out_shape=[
            jax.ShapeDtypeStruct((batch, seq, D_POOL), f32),
            jax.ShapeDtypeStruct((batch, seq, 3 * D_ATTN), bf16),
        ],
        compiler_params=_params(),
        name="in_proj",
    )(x, mod, w_in)


def _pool_tile(j, u_ref, halo_ref, wp_ref, ps_ref, y_ref):
    halo = jnp.where(j > 0, halo_ref[0], 0.0)
    t_head = j * SEQ_TILE + lax.broadcasted_iota(jnp.int32, (MAX_WINDOW, POOL_GROUP_DIM), 0)
    for g, w in enumerate(POOL_WINDOWS):
        cols = slice(g * POOL_GROUP_DIM, (g + 1) * POOL_GROUP_DIM)
        tok = u_ref[0, :, cols]
        win = jnp.concatenate([halo[:, cols], tok], axis=0)
        span = 1
        while span < w:
            win = win + pltpu.roll(win, span, axis=0)
            span *= 2
        win = win[MAX_WINDOW:]
        head = win[:MAX_WINDOW] / jnp.minimum(t_head + 1, w).astype(f32) - tok[:MAX_WINDOW]
        rest = win[MAX_WINDOW:] * (1.0 / w) - tok[MAX_WINDOW:]
        pooled = jnp.concatenate([head, rest], axis=0)
        yp = jnp.dot(pooled.astype(bf16), wp_ref[0, g], preferred_element_type=f32)
        y_ref[0, :, cols] = (yp * ps_ref[0, :, cols]).astype(bf16)


def _attn_tile(first_tile, q_ref, k_ref, v_ref, pk_ref, pv_ref, bias_ref, y_ref, s_scr, m_scr):
    lane = lax.broadcasted_iota(jnp.int32, (CHUNK, LANES), 1)
    first_head = lane < HEAD_DIM
    nt = (((1,), (1,)), ((), ()))

    def extents(c):
        a0 = LANES * (c // 2)
        b1 = LANES * (c // 2 + 1)
        w_prev = 0 if first_tile else SEQ_TILE - a0
        return a0, b1, w_prev

    def scores(c):
        a0, b1, w_prev = extents(c)
        b_lo = SEQ_TILE - a0 - w_prev
        for p in range(N_HEAD_GROUPS):
            grp = slice(p * LANES, (p + 1) * LANES)
            qp = q_ref[0, c * CHUNK:(c + 1) * CHUNK, grp]
            zero = jnp.zeros_like(qp)
            qs = jnp.concatenate([jnp.where(first_head, qp, zero),
                                  jnp.where(first_head, zero, qp)], axis=0)
            s = lax.dot_general(qs, k_ref[0, 0:b1, grp], nt, preferred_element_type=f32)
            if w_prev:
                s = jnp.concatenate(
                    [lax.dot_general(qs, pk_ref[0, a0:SEQ_TILE, grp], nt,
                                     preferred_element_type=f32), s], axis=1)
            s = s + bias_ref[0, c % 2, p, :, b_lo:b_lo + w_prev + b1]
            s_scr[c % 2, p, :, 0:w_prev + b1] = s
            m_scr[c % 2, p] = jnp.broadcast_to(jnp.max(s, axis=-1, keepdims=True),
                                               (GROUP_ROWS, LANES))

    def normalise(c):
        a0, b1, w_prev = extents(c)
        width = w_prev + b1
        for p in range(N_HEAD_GROUPS):
            grp = slice(p * LANES, (p + 1) * LANES)
            m = m_scr[c % 2, p]
            e = jnp.concatenate(
                [jnp.exp2(s_scr[c % 2, p, :, lo:lo + LANES] - m) for lo in range(0, width, LANES)],
                axis=1)
            denom = jnp.sum(e, axis=-1, keepdims=True)
            eb = e.astype(bf16)
            o = jnp.dot(eb[:, w_prev:], v_ref[0, 0:b1, grp], preferred_element_type=f32)
            if w_prev:
                o = o + jnp.dot(eb[:, :w_prev], pv_ref[0, a0:SEQ_TILE, grp],
                                preferred_element_type=f32)
            o = o / denom
            out = jnp.where(first_head, o[0:CHUNK], o[CHUNK:GROUP_ROWS])
            y_ref[0, c * CHUNK:(c + 1) * CHUNK, D_POOL + p * LANES:D_POOL + (p + 1) * LANES] = (
                out.astype(bf16))

    scores(0)
    for c in range(CHUNKS_PER_TILE):
        if c + 1 < CHUNKS_PER_TILE:
            scores(c + 1)
        normalise(c)


def _mixer_kernel(u_ref, halo_ref, q_ref, k_ref, v_ref, pk_ref, pv_ref, bias_ref, wp_ref, ps_ref,
                  y_ref, s_scr, m_scr):
    j = pl.program_id(1)
    _pool_tile(j, u_ref, halo_ref, wp_ref, ps_ref, y_ref)
    attn = functools.partial(_attn_tile, q_ref=q_ref, k_ref=k_ref, v_ref=v_ref, pk_ref=pk_ref,
                             pv_ref=pv_ref, bias_ref=bias_ref, y_ref=y_ref, s_scr=s_scr,
                             m_scr=m_scr)
    pl.when(j == 0)(functools.partial(attn, True))
    pl.when(j > 0)(functools.partial(attn, False))


def _mixer_call(l, u, qkv, bias, w_pool, pool_scale):
    batch, seq, _ = u.shape
    halo_blocks = SEQ_TILE // MAX_WINDOW
    cur = lambda part: pl.BlockSpec((1, SEQ_TILE, D_ATTN), lambda b, j: (b, j, part))
    prev = lambda part: pl.BlockSpec((1, SEQ_TILE, D_ATTN),
                                     lambda b, j: (b, jnp.maximum(j - 1, 0), part))
    return pl.pallas_call(
        _mixer_kernel,
        grid=(batch, seq // SEQ_TILE),
        in_specs=[
            pl.BlockSpec((1, SEQ_TILE, D_POOL), lambda b, j: (b, j, 0)),
            pl.BlockSpec((1, MAX_WINDOW, D_POOL),
                         lambda b, j: (b, jnp.maximum(j * halo_blocks - 1, 0), 0)),
            cur(0), cur(1), cur(2), prev(1), prev(2),
            _layer_block(l, bias.shape[1:]),
            _layer_block(l, w_pool.shape[1:]),
            _layer_block(l, pool_scale.shape[1:]),
        ],
        out_specs=pl.BlockSpec((1, SEQ_TILE, D_MODEL), lambda b, j: (b, j, 0)),
        out_shape=jax.ShapeDtypeStruct((batch, seq, D_MODEL), bf16),
        scratch_shapes=[pltpu.VMEM((2, N_HEAD_GROUPS, GROUP_ROWS, BAND_PAD), f32),
                        pltpu.VMEM((2, N_HEAD_GROUPS, GROUP_ROWS, LANES), f32)],
        compiler_params=_params(),
        name="mixer",
    )(u, u, qkv, qkv, qkv, qkv, qkv, bias, w_pool, pool_scale)


def _out_ffn_kernel(alpha, y_ref, x_ref, mod_ref, wo_ref, ln1_ref, w1_ref, w2_ref, ln2_ref, o_ref):
    g1 = mod_ref[0, 0, 2:3, :]
    sh2 = mod_ref[0, 0, 3:4, :]
    sc2 = mod_ref[0, 0, 4:5, :]
    g2 = mod_ref[0, 0, 5:6, :]
    chains = [slice(i * ROW_TILE, (i + 1) * ROW_TILE) for i in range(FFN_CHAINS)]
    a = [jnp.dot(y_ref[0, rows, :], wo_ref[0], preferred_element_type=f32) for rows in chains]
    x1 = [_layer_norm(alpha * x_ref[0, rows, :] + (1.0 + g1) * a_i,
                      ln1_ref[0, 0:1, :], ln1_ref[0, 1:2, :]) for rows, a_i in zip(chains, a)]
    h = [(x1_i * (1.0 + sc2) + sh2).astype(bf16) for x1_i in x1]
    acc = [None] * FFN_CHAINS
    for s in range(D_FF // FF_COLS):
        cols = slice(s * FF_COLS, (s + 1) * FF_COLS)
        for i in range(FFN_CHAINS):
            f = jnp.dot(h[i], w1_ref[0, :, cols], preferred_element_type=f32)
            f = jnp.square(jnp.maximum(f, 0.0)).astype(bf16)
            part = jnp.dot(f, w2_ref[0, cols, :], preferred_element_type=f32)
            acc[i] = part if acc[i] is None else acc[i] + part
    for i, rows in enumerate(chains):
        o_ref[0, rows, :] = _layer_norm(alpha * x1[i] + (1.0 + g2) * acc[i],
                                        ln2_ref[0, 0:1, :], ln2_ref[0, 1:2, :])


def _out_ffn_call(l, alpha, y, x, mod, w_out, ln1, w_ff1, w_ff2, ln2):
    batch, seq, d = x.shape
    tile = lambda b, i: (b, i, 0)
    rows = FFN_CHAINS * ROW_TILE
    return pl.pallas_call(
        functools.partial(_out_ffn_kernel, alpha),
        grid=(batch, seq // rows),
        in_specs=[
            pl.BlockSpec((1, rows, d), tile),
            pl.BlockSpec((1, rows, d), tile),
            pl.BlockSpec((1, 1, N_MOD, d), lambda b, i: (l, b, 0, 0)),
            _layer_block(l, (d, d)),
            _layer_block(l, (2, d)),
            _layer_block(l, (d, D_FF)),
            _layer_block(l, (D_FF, d)),
            _layer_block(l, (2, d)),
        ],
        out_specs=pl.BlockSpec((1, rows, d), tile),
        out_shape=jax.ShapeDtypeStruct((batch, seq, d), f32),
        compiler_params=_params(),
        name="out_ffn",
    )(y, x, mod, w_out, ln1, w_ff1, w_ff2, ln2)


def _band_bias_tables(rel_bias):
    tab = rel_bias.astype(f32) * LOG2E
    lead = tab.shape[:-1]
    far = tab[..., -1:]
    n_near = REL_MAX + CHUNK
    period = n_near + CHUNK
    ring = jnp.concatenate([tab[..., ::-1], jnp.broadcast_to(far, (*lead, period - N_REL))], -1)
    near = jnp.tile(ring, (1, 1, CHUNK))[..., :CHUNK * (period - 1)]
    near = near.reshape(*lead, CHUNK, period - 1)[..., :n_near]
    clipped = jnp.broadcast_to(far[..., None], (*lead, CHUNK, BAND_LEN - n_near))
    band = jnp.concatenate([clipped, near], axis=-1)
    masked = jnp.full((*lead, CHUNK, CHUNK), MASK_VALUE, f32)
    tables = jnp.stack([jnp.concatenate([band, masked], -1),
                        jnp.concatenate([masked, band], -1)], axis=1)
    return tables.reshape(lead[0], 2, N_HEAD_GROUPS, GROUP_ROWS, BAND_PAD)


def kernel(x, c, w_ada, b_ada, w_in, w_pool, pool_scale, rel_bias, w_out, ln1_g, ln1_b,
           w_ff1, w_ff2, ln2_g, ln2_b):
    depth = w_in.shape[0]
    batch = x.shape[0]
    alpha = (2.0 * depth) ** 0.25
    mod = _ada_call(c, w_ada, b_ada).reshape(depth, batch, N_MOD, D_MODEL)
    bias = _band_bias_tables(rel_bias)
    w_in, w_pool, w_out, w_ff1, w_ff2 = (w.astype(bf16) for w in (w_in, w_pool, w_out, w_ff1, w_ff2))
    pool_scale = pool_scale.reshape(depth, 1, D_POOL)
    ln1 = jnp.stack([ln1_g, ln1_b], axis=1)
    ln2 = jnp.stack([ln2_g, ln2_b], axis=1)
    for l in range(depth):
        u, qkv = _in_proj_call(l, x, mod, w_in)
        y = _mixer_call(l, u, qkv, bias, w_pool, pool_scale)
        x = _out_ffn_call(l, alpha, y, x, mod, w_out, ln1, w_ff1, w_ff2, ln2)
    return x
```

```python
import functools
import math

import jax
import jax.numpy as jnp
from jax import lax
from jax.experimental import pallas as pl
from jax.experimental.pallas import tpu as pltpu

D_MODEL = 1024
CHUNK = 64
D_POOL = D_MODEL // 2
POOL_WINDOWS = (2, 4, 8, 16)
POOL_GROUP_DIM = D_POOL // len(POOL_WINDOWS)
D_ATTN = D_MODEL - D_POOL
N_HEADS = 8
HEAD_DIM = D_ATTN // N_HEADS
LEFT_CHUNKS = 8
BAND = LEFT_CHUNKS + 1
REL_MAX = 128
REL_MIN = CHUNK - 1
N_REL = REL_MIN + REL_MAX + 1
D_FF = 4 * D_MODEL
D_IN = D_POOL + 3 * D_ATTN
N_MOD = 6
LN_EPS = 1e-5
MASK_VALUE = -1e30
LOG2E = math.log2(math.e)

LANES = 128
BAND_LEN = BAND * CHUNK
BAND_PAD = BAND_LEN + CHUNK
HEADS_PER_GROUP = LANES // HEAD_DIM
N_HEAD_GROUPS = N_HEADS // HEADS_PER_GROUP
GROUP_ROWS = HEADS_PER_GROUP * CHUNK
MAX_WINDOW = max(POOL_WINDOWS)

SEQ_TILE = LEFT_CHUNKS * CHUNK
CHUNKS_PER_TILE = SEQ_TILE // CHUNK
ROW_TILE = 512
FFN_CHAINS = 2
IN_ROWS = 1024
ADA_COLS = 1536
FF_COLS = 1024

VMEM_LIMIT = 56 * 1024 * 1024

f32 = jnp.float32
bf16 = jnp.bfloat16


def _layer_block(l, shape):
    return pl.BlockSpec((1, *shape), lambda *_: (l,) + (0,) * len(shape),
                        pipeline_mode=pl.Buffered(1))


def _params():
    return pltpu.CompilerParams(dimension_semantics=("arbitrary", "arbitrary"),
                                vmem_limit_bytes=VMEM_LIMIT)


def _layer_norm(z, g, b):
    mu = jnp.mean(z, axis=-1, keepdims=True)
    zc = z - mu
    var = jnp.mean(zc * zc, axis=-1, keepdims=True)
    return zc * lax.rsqrt(var + LN_EPS) * g + b


def _ada_kernel(c_ref, w_ref, b_ref, o_ref):
    c = c_ref[...]
    c_act = c / (1.0 + jnp.exp(-c))
    o_ref[0] = jnp.dot(c_act, w_ref[0], preferred_element_type=f32) + b_ref[0]


def _ada_call(c, w_ada, b_ada):
    depth, d, n = w_ada.shape
    batch = c.shape[0]
    return pl.pallas_call(
        _ada_kernel,
        grid=(depth, n // ADA_COLS),
        in_specs=[
            pl.BlockSpec((batch, d), lambda l, j: (0, 0)),
            pl.BlockSpec((1, d, ADA_COLS), lambda l, j: (l, 0, j)),
            pl.BlockSpec((1, 1, ADA_COLS), lambda l, j: (l, 0, j)),
        ],
        out_specs=pl.BlockSpec((1, batch, ADA_COLS), lambda l, j: (l, 0, j)),
        out_shape=jax.ShapeDtypeStruct((depth, batch, n), f32),
        compiler_params=_params(),
        name="ada_mod",
    )(c, w_ada, b_ada.reshape(depth, 1, n))


def _pool_rows(first_frame, u, halo, wp_ref, ps_ref, y_ref, rows):
    t_head = first_frame + lax.broadcasted_iota(jnp.int32, (MAX_WINDOW, POOL_GROUP_DIM), 0)
    pooled_groups = []
    for g, w in enumerate(POOL_WINDOWS):
        cols = slice(g * POOL_GROUP_DIM, (g + 1) * POOL_GROUP_DIM)
        tok = u[:, cols]
        win = jnp.concatenate([halo[:, cols], tok], axis=0)
        span = 1
        while span < w:
            win = win + pltpu.roll(win, span, axis=0)
            span *= 2
        win = win[MAX_WINDOW:]
        head = win[:MAX_WINDOW] / jnp.minimum(t_head + 1, w).astype(f32) - tok[:MAX_WINDOW]
        rest = win[MAX_WINDOW:] * (1.0 / w) - tok[MAX_WINDOW:]
        pooled_groups.append(jnp.concatenate([head, rest], axis=0).astype(bf16))
    for k in range(len(POOL_WINDOWS) // 2):
        cols = slice(2 * k * POOL_GROUP_DIM, 2 * (k + 1) * POOL_GROUP_DIM)
        pooled = jnp.concatenate(pooled_groups[2 * k:2 * k + 2], axis=1)
        yp = jnp.dot(pooled, wp_ref[0, k], preferred_element_type=f32)
        y_ref[0, rows, cols] = (yp * ps_ref[0, :, cols]).astype(bf16)


def _in_proj_kernel(x_ref, mod_ref, w_ref, wp_ref, ps_ref, yp_ref, qkv_ref, halo_scr):
    i = pl.program_id(1)

    @pl.when(i == 0)
    def _():
        halo_scr[...] = jnp.zeros((MAX_WINDOW, D_POOL), f32)

    sh1 = mod_ref[0, 0, 0:1, :]
    sc1 = mod_ref[0, 0, 1:2, :]
    halo = halo_scr[...]
    for r in range(0, IN_ROWS, ROW_TILE):
        rows = slice(r, r + ROW_TILE)
        h = (x_ref[0, rows, :] * (1.0 + sc1) + sh1).astype(bf16)
        u = jnp.dot(h, w_ref[0, :, 0:D_POOL], preferred_element_type=f32)
        _pool_rows(i * IN_ROWS + r, u, halo, wp_ref, ps_ref, yp_ref, rows)
        halo = u[ROW_TILE - MAX_WINDOW:]
        q = jnp.dot(h, w_ref[0, :, D_POOL:D_POOL + D_ATTN], preferred_element_type=f32)
        qkv_ref[0, rows, 0:D_ATTN] = (q * (HEAD_DIM ** -0.5 * LOG2E)).astype(bf16)
        for n in (1, 2):
            lo = D_POOL + n * D_ATTN
            kv = jnp.dot(h, w_ref[0, :, lo:lo + D_ATTN], preferred_element_type=f32)
            qkv_ref[0, rows, n * D_ATTN:(n + 1) * D_ATTN] = kv.astype(bf16)
    halo_scr[...] = halo


def _in_proj_call(l, x, mod, w_in, w_pool, pool_scale):
    batch, seq, d = x.shape
    return pl.pallas_call(
        _in_proj_kernel,
        grid=(batch, seq // IN_ROWS),
        in_specs=[
            pl.BlockSpec((1, IN_ROWS, d), lambda b, i: (b, i, 0)),
            pl.BlockSpec((1, 1, N_MOD, d), lambda b, i: (l, b, 0, 0)),
            _layer_block(l, (d, D_IN)),
            _layer_block(l, w_pool.shape[1:]),
            _layer_block(l, pool_scale.shape[1:]),
        ],
        out_specs=[
            pl.BlockSpec((1, IN_ROWS, D_POOL), lambda b, i: (b, i, 0)),
            pl.BlockSpec((1, IN_ROWS, 3 * D_ATTN), lambda b, i: (b, i, 0)),
        ],
        out_shape=[
            jax.ShapeDtypeStruct((batch, seq, D_POOL), bf16),
            jax.ShapeDtypeStruct((batch, seq, 3 * D_ATTN), bf16),
        ],
        scratch_shapes=[pltpu.VMEM((MAX_WINDOW, D_POOL), f32)],
        compiler_params=_params(),
        name="in_proj",
    )(x, mod, w_in, w_pool, pool_scale)


def _attn_tile(first_tile, q_ref, k_ref, v_ref, pk_ref, pv_ref, bias_ref, y_ref, s_scr, m_scr, d_scr,
               p_scr):
    lane = lax.broadcasted_iota(jnp.int32, (CHUNK, LANES), 1)
    first_head = lane < HEAD_DIM
    nt = (((1,), (1,)), ((), ()))

    def extents(c):
        a0 = LANES * (c // 2)
        b1 = LANES * (c // 2 + 1)
        w_prev = 0 if first_tile else SEQ_TILE - a0
        return a0, b1, w_prev

    def scores(c):
        a0, b1, w_prev = extents(c)
        b_lo = SEQ_TILE - a0 - w_prev
        for p in range(N_HEAD_GROUPS):
            grp = slice(p * LANES, (p + 1) * LANES)
            qp = q_ref[0, c * CHUNK:(c + 1) * CHUNK, grp]
            zero = jnp.zeros_like(qp)
            qs = jnp.concatenate([jnp.where(first_head, qp, zero),
                                  jnp.where(first_head, zero, qp)], axis=0)
            s = lax.dot_general(qs, k_ref[0, 0:b1, grp], nt, preferred_element_type=f32)
            if w_prev:
                s = jnp.concatenate(
                    [lax.dot_general(qs, pk_ref[0, a0:SEQ_TILE, grp], nt,
                                     preferred_element_type=f32), s], axis=1)
            s = s + bias_ref[0, c % 2, p, :, b_lo:b_lo + w_prev + b1]
            s_scr[c % 2, p, :, 0:w_prev + b1] = s
            m_scr[c % 2, p] = jnp.broadcast_to(jnp.max(s, axis=-1, keepdims=True),
                                               (GROUP_ROWS, LANES))

    def exponentiate(c):
        a0, b1, w_prev = extents(c)
        width = w_prev + b1
        for p in range(N_HEAD_GROUPS):
            m = m_scr[c % 2, p]
            e = jnp.concatenate(
                [jnp.exp2(s_scr[c % 2, p, :, lo:lo + LANES] - m) for lo in range(0, width, LANES)],
                axis=1)
            d_scr[c % 2, p] = jnp.broadcast_to(jnp.sum(e, axis=-1, keepdims=True),
                                               (GROUP_ROWS, LANES))
            p_scr[c % 2, p, :, 0:width] = e.astype(bf16)

    def attend(c):
        a0, b1, w_prev = extents(c)
        width = w_prev + b1
        for p in range(N_HEAD_GROUPS):
            grp = slice(p * LANES, (p + 1) * LANES)
            o = jnp.dot(p_scr[c % 2, p, :, w_prev:width], v_ref[0, 0:b1, grp],
                        preferred_element_type=f32)
            if w_prev:
                o = o + jnp.dot(p_scr[c % 2, p, :, 0:w_prev], pv_ref[0, a0:SEQ_TILE, grp],
                                preferred_element_type=f32)
            o = o / d_scr[c % 2, p]
            out = jnp.where(first_head, o[0:CHUNK], o[CHUNK:GROUP_ROWS])
            y_ref[0, c * CHUNK:(c + 1) * CHUNK, grp] = out.astype(bf16)

    for t in range(CHUNKS_PER_TILE + 2):
        if t < CHUNKS_PER_TILE:
            scores(t)
        if 1 <= t <= CHUNKS_PER_TILE:
            exponentiate(t - 1)
        if t >= 2:
            attend(t - 2)


def _mixer_kernel(q_ref, k_ref, v_ref, pk_ref, pv_ref, bias_ref, y_ref, s_scr, m_scr, d_scr, p_scr):
    j = pl.program_id(1)
    attn = functools.partial(_attn_tile, q_ref=q_ref, k_ref=k_ref, v_ref=v_ref, pk_ref=pk_ref,
                             pv_ref=pv_ref, bias_ref=bias_ref, y_ref=y_ref, s_scr=s_scr,
                             m_scr=m_scr, d_scr=d_scr, p_scr=p_scr)
    pl.when(j == 0)(functools.partial(attn, True))
    pl.when(j > 0)(functools.partial(attn, False))


def _mixer_call(l, qkv, bias):
    batch, seq, _ = qkv.shape
    cur = lambda part: pl.BlockSpec((1, SEQ_TILE, D_ATTN), lambda b, j: (b, j, part))
    prev = lambda part: pl.BlockSpec((1, SEQ_TILE, D_ATTN),
                                     lambda b, j: (b, jnp.maximum(j - 1, 0), part))
    return pl.pallas_call(
        _mixer_kernel,
        grid=(batch, seq // SEQ_TILE),
        in_specs=[cur(0), cur(1), cur(2), prev(1), prev(2), _layer_block(l, bias.shape[1:])],
        out_specs=pl.BlockSpec((1, SEQ_TILE, D_ATTN), lambda b, j: (b, j, 0)),
        out_shape=jax.ShapeDtypeStruct((batch, seq, D_ATTN), bf16),
        scratch_shapes=[pltpu.VMEM((2, N_HEAD_GROUPS, GROUP_ROWS, BAND_PAD), f32),
                        pltpu.VMEM((2, N_HEAD_GROUPS, GROUP_ROWS, LANES), f32),
                        pltpu.VMEM((2, N_HEAD_GROUPS, GROUP_ROWS, LANES), f32),
                        pltpu.VMEM((2, N_HEAD_GROUPS, GROUP_ROWS, BAND_PAD), bf16)],
        compiler_params=_params(),
        name="mixer",
    )(qkv, qkv, qkv, qkv, qkv, bias)


def _out_ffn_kernel(alpha, yp_ref, ya_ref, x_ref, mod_ref, wo_ref, ln1_ref, w1_ref, w2_ref, ln2_ref,
                    o_ref):
    g1 = mod_ref[0, 0, 2:3, :]
    sh2 = mod_ref[0, 0, 3:4, :]
    sc2 = mod_ref[0, 0, 4:5, :]
    g2 = mod_ref[0, 0, 5:6, :]
    chains = [slice(i * ROW_TILE, (i + 1) * ROW_TILE) for i in range(FFN_CHAINS)]
    a = [jnp.dot(yp_ref[0, rows, :], wo_ref[0, 0:D_POOL, :], preferred_element_type=f32)
         + jnp.dot(ya_ref[0, rows, :], wo_ref[0, D_POOL:D_MODEL, :], preferred_element_type=f32)
         for rows in chains]
    x1 = [_layer_norm(alpha * x_ref[0, rows, :] + (1.0 + g1) * a_i,
                      ln1_ref[0, 0:1, :], ln1_ref[0, 1:2, :]) for rows, a_i in zip(chains, a)]
    h = [(x1_i * (1.0 + sc2) + sh2).astype(bf16) for x1_i in x1]
    acc = [None] * FFN_CHAINS
    for s in range(D_FF // FF_COLS):
        cols = slice(s * FF_COLS, (s + 1) * FF_COLS)
        for i in range(FFN_CHAINS):
            f = jnp.dot(h[i], w1_ref[0, :, cols], preferred_element_type=f32)
            f = jnp.square(jnp.maximum(f, 0.0)).astype(bf16)
            part = jnp.dot(f, w2_ref[0, cols, :], preferred_element_type=f32)
            acc[i] = part if acc[i] is None else acc[i] + part
    for i, rows in enumerate(chains):
        o_ref[0, rows, :] = _layer_norm(alpha * x1[i] + (1.0 + g2) * acc[i],
                                        ln2_ref[0, 0:1, :], ln2_ref[0, 1:2, :])


def _out_ffn_call(l, alpha, y_pool, y_attn, x, mod, w_out, ln1, w_ff1, w_ff2, ln2):
    batch, seq, d = x.shape
    tile = lambda b, i: (b, i, 0)
    rows = FFN_CHAINS * ROW_TILE
    return pl.pallas_call(
        functools.partial(_out_ffn_kernel, alpha),
        grid=(batch, seq // rows),
        in_specs=[
            pl.BlockSpec((1, rows, D_POOL), tile),
            pl.BlockSpec((1, rows, D_ATTN), tile),
            pl.BlockSpec((1, rows, d), tile),
            pl.BlockSpec((1, 1, N_MOD, d), lambda b, i: (l, b, 0, 0)),
            _layer_block(l, (d, d)),
            _layer_block(l, (2, d)),
            _layer_block(l, (d, D_FF)),
            _layer_block(l, (D_FF, d)),
            _layer_block(l, (2, d)),
        ],
        out_specs=pl.BlockSpec((1, rows, d), tile),
        out_shape=jax.ShapeDtypeStruct((batch, seq, d), f32),
        compiler_params=_params(),
        name="out_ffn",
    )(y_pool, y_attn, x, mod, w_out, ln1, w_ff1, w_ff2, ln2)


def _band_bias_tables(rel_bias):
    tab = rel_bias.astype(f32) * LOG2E
    lead = tab.shape[:-1]
    far = tab[..., -1:]
    n_near = REL_MAX + CHUNK
    period = n_near + CHUNK
    ring = jnp.concatenate([tab[..., ::-1], jnp.broadcast_to(far, (*lead, period - N_REL))], -1)
    near = jnp.tile(ring, (1, 1, CHUNK))[..., :CHUNK * (period - 1)]
    near = near.reshape(*lead, CHUNK, period - 1)[..., :n_near]
    clipped = jnp.broadcast_to(far[..., None], (*lead, CHUNK, BAND_LEN - n_near))
    band = jnp.concatenate([clipped, near], axis=-1)
    masked = jnp.full((*lead, CHUNK, CHUNK), MASK_VALUE, f32)
    tables = jnp.stack([jnp.concatenate([band, masked], -1),
                        jnp.concatenate([masked, band], -1)], axis=1)
    return tables.reshape(lead[0], 2, N_HEAD_GROUPS, GROUP_ROWS, BAND_PAD)


def _pair_block_diagonal(w_pool):
    depth, groups, n, _ = w_pool.shape
    w = w_pool.reshape(depth, groups // 2, 2, n, n)
    zero = jnp.zeros_like(w[:, :, 0])
    top = jnp.concatenate([w[:, :, 0], zero], axis=-1)
    bottom = jnp.concatenate([zero, w[:, :, 1]], axis=-1)
    return jnp.concatenate([top, bottom], axis=-2)


def kernel(x, c, w_ada, b_ada, w_in, w_pool, pool_scale, rel_bias, w_out, ln1_g, ln1_b,
           w_ff1, w_ff2, ln2_g, ln2_b):
    depth = w_in.shape[0]
    batch = x.shape[0]
    alpha = (2.0 * depth) ** 0.25
    mod = _ada_call(c, w_ada, b_ada).reshape(depth, batch, N_MOD, D_MODEL)
    bias = _band_bias_tables(rel_bias)
    w_in, w_pool, w_out, w_ff1, w_ff2 = (w.astype(bf16) for w in (w_in, w_pool, w_out, w_ff1, w_ff2))
    w_pool = _pair_block_diagonal(w_pool)
    pool_scale = pool_scale.reshape(depth, 1, D_POOL)
    ln1 = jnp.stack([ln1_g, ln1_b], axis=1)
    ln2 = jnp.stack([ln2_g, ln2_b], axis=1)
    for l in range(depth):
        y_pool, qkv = _in_proj_call(l, x, mod, w_in, w_pool, pool_scale)
        y_attn = _mixer_call(l, qkv, bias)
        x = _out_ffn_call(l, alpha, y_pool, y_attn, x, mod, w_out, ln1, w_ff1, w_ff2, ln2)
    return x
```

```python
import functools
import math

import jax
import jax.numpy as jnp
from jax import lax
from jax.experimental import pallas as pl
from jax.experimental.pallas import tpu as pltpu

D_MODEL = 1024
CHUNK = 64
D_POOL = D_MODEL // 2
POOL_WINDOWS = (2, 4, 8, 16)
POOL_GROUP_DIM = D_POOL // len(POOL_WINDOWS)
D_ATTN = D_MODEL - D_POOL
N_HEADS = 8
HEAD_DIM = D_ATTN // N_HEADS
LEFT_CHUNKS = 8
BAND = LEFT_CHUNKS + 1
REL_MAX = 128
REL_MIN = CHUNK - 1
N_REL = REL_MIN + REL_MAX + 1
D_FF = 4 * D_MODEL
D_IN = D_POOL + 3 * D_ATTN
N_MOD = 6
LN_EPS = 1e-5
MASK_VALUE = -1e30
LOG2E = math.log2(math.e)

LANES = 128
BAND_LEN = BAND * CHUNK
BAND_PAD = BAND_LEN + CHUNK
HEADS_PER_GROUP = LANES // HEAD_DIM
N_HEAD_GROUPS = N_HEADS // HEADS_PER_GROUP
GROUP_ROWS = HEADS_PER_GROUP * CHUNK
MAX_WINDOW = max(POOL_WINDOWS)

SEQ_TILE = LEFT_CHUNKS * CHUNK
CHUNKS_PER_TILE = SEQ_TILE // CHUNK
ROW_TILE = 512
FFN_CHAINS = 2
IN_ROWS = 1024
ADA_COLS = 1536
FF_COLS = 1024

VMEM_LIMIT = 56 * 1024 * 1024

f32 = jnp.float32
bf16 = jnp.bfloat16


def _layer_block(l, shape):
    return pl.BlockSpec((1, *shape), lambda *_: (l,) + (0,) * len(shape),
                        pipeline_mode=pl.Buffered(1))


def _params():
    return pltpu.CompilerParams(dimension_semantics=("arbitrary", "arbitrary"),
                                vmem_limit_bytes=VMEM_LIMIT)


def _layer_norm(z, g, b):
    mu = jnp.mean(z, axis=-1, keepdims=True)
    zc = z - mu
    var = jnp.mean(zc * zc, axis=-1, keepdims=True)
    return zc * lax.rsqrt(var + LN_EPS) * g + b


def _ada_kernel(c_ref, w_ref, b_ref, o_ref):
    c = c_ref[...]
    c_act = c / (1.0 + jnp.exp(-c))
    o_ref[0] = jnp.dot(c_act, w_ref[0], preferred_element_type=f32) + b_ref[0]


def _ada_call(c, w_ada, b_ada):
    depth, d, n = w_ada.shape
    batch = c.shape[0]
    return pl.pallas_call(
        _ada_kernel,
        grid=(depth, n // ADA_COLS),
        in_specs=[
            pl.BlockSpec((batch, d), lambda l, j: (0, 0)),
            pl.BlockSpec((1, d, ADA_COLS), lambda l, j: (l, 0, j)),
            pl.BlockSpec((1, 1, ADA_COLS), lambda l, j: (l, 0, j)),
        ],
        out_specs=pl.BlockSpec((1, batch, ADA_COLS), lambda l, j: (l, 0, j)),
        out_shape=jax.ShapeDtypeStruct((depth, batch, n), f32),
        compiler_params=_params(),
        name="ada_mod",
    )(c, w_ada, b_ada.reshape(depth, 1, n))


def _pool_rows(first_frame, u, halo, wp_ref, ps_ref, y_ref, rows):
    t_head = first_frame + lax.broadcasted_iota(jnp.int32, (MAX_WINDOW, POOL_GROUP_DIM), 0)
    pooled_groups = []
    for g, w in enumerate(POOL_WINDOWS):
        cols = slice(g * POOL_GROUP_DIM, (g + 1) * POOL_GROUP_DIM)
        tok = u[:, cols]
        win = jnp.concatenate([halo[:, cols], tok], axis=0)
        span = 1
        while span < w:
            win = win + pltpu.roll(win, span, axis=0)
            span *= 2
        win = win[MAX_WINDOW:]
        head = win[:MAX_WINDOW] / jnp.minimum(t_head + 1, w).astype(f32) - tok[:MAX_WINDOW]
        rest = win[MAX_WINDOW:] * (1.0 / w) - tok[MAX_WINDOW:]
        pooled_groups.append(jnp.concatenate([head, rest], axis=0).astype(bf16))
    for k in range(len(POOL_WINDOWS) // 2):
        cols = slice(2 * k * POOL_GROUP_DIM, 2 * (k + 1) * POOL_GROUP_DIM)
        pooled = jnp.concatenate(pooled_groups[2 * k:2 * k + 2], axis=1)
        yp = jnp.dot(pooled, wp_ref[0, k], preferred_element_type=f32)
        y_ref[0, rows, cols] = (yp * ps_ref[0, :, cols]).astype(bf16)


def _in_proj_kernel(x_ref, mod_ref, w_ref, wkt_ref, wp_ref, ps_ref, yp_ref, q_ref, kt_ref, v_ref,
                    halo_scr):
    i = pl.program_id(1)

    @pl.when(i == 0)
    def _():
        halo_scr[...] = jnp.zeros((MAX_WINDOW, D_POOL), f32)

    sh1 = mod_ref[0, 0, 0:1, :]
    sc1 = mod_ref[0, 0, 1:2, :]
    halo = halo_scr[...]
    for r in range(0, IN_ROWS, ROW_TILE):
        rows = slice(r, r + ROW_TILE)
        h = (x_ref[0, rows, :] * (1.0 + sc1) + sh1).astype(bf16)
        u = jnp.dot(h, w_ref[0, :, 0:D_POOL], preferred_element_type=f32)
        _pool_rows(i * IN_ROWS + r, u, halo, wp_ref, ps_ref, yp_ref, rows)
        halo = u[ROW_TILE - MAX_WINDOW:]
        q = jnp.dot(h, w_ref[0, :, D_POOL:D_POOL + D_ATTN], preferred_element_type=f32)
        q_ref[0, rows, :] = (q * (HEAD_DIM ** -0.5 * LOG2E)).astype(bf16)
        kt = lax.dot_general(wkt_ref[0], h, (((1,), (1,)), ((), ())), preferred_element_type=f32)
        kt_ref[0, :, rows] = kt.astype(bf16)
        v = jnp.dot(h, w_ref[0, :, D_POOL + 2 * D_ATTN:D_IN], preferred_element_type=f32)
        v_ref[0, rows, :] = v.astype(bf16)
    halo_scr[...] = halo


def _in_proj_call(l, x, mod, w_in, w_kt, w_pool, pool_scale):
    batch, seq, d = x.shape
    return pl.pallas_call(
        _in_proj_kernel,
        grid=(batch, seq // IN_ROWS),
        in_specs=[
            pl.BlockSpec((1, IN_ROWS, d), lambda b, i: (b, i, 0)),
            pl.BlockSpec((1, 1, N_MOD, d), lambda b, i: (l, b, 0, 0)),
            _layer_block(l, (d, D_IN)),
            _layer_block(l, (D_ATTN, d)),
            _layer_block(l, w_pool.shape[1:]),
            _layer_block(l, pool_scale.shape[1:]),
        ],
        out_specs=[
            pl.BlockSpec((1, IN_ROWS, D_POOL), lambda b, i: (b, i, 0)),
            pl.BlockSpec((1, IN_ROWS, D_ATTN), lambda b, i: (b, i, 0)),
            pl.BlockSpec((1, D_ATTN, IN_ROWS), lambda b, i: (b, 0, i)),
            pl.BlockSpec((1, IN_ROWS, D_ATTN), lambda b, i: (b, i, 0)),
        ],
        out_shape=[
            jax.ShapeDtypeStruct((batch, seq, D_POOL), bf16),
            jax.ShapeDtypeStruct((batch, seq, D_ATTN), bf16),
            jax.ShapeDtypeStruct((batch, D_ATTN, seq), bf16),
            jax.ShapeDtypeStruct((batch, seq, D_ATTN), bf16),
        ],
        scratch_shapes=[pltpu.VMEM((MAX_WINDOW, D_POOL), f32)],
        compiler_params=_params(),
        name="in_proj",
    )(x, mod, w_in, w_kt, w_pool, pool_scale)


def _attn_tile(first_tile, q_ref, kt_ref, v_ref, pkt_ref, pv_ref, bias_ref, y_ref, s_scr, m_scr,
               d_scr, p_scr):
    lane = lax.broadcasted_iota(jnp.int32, (CHUNK, LANES), 1)
    first_head = lane < HEAD_DIM

    def extents(c):
        a0 = LANES * (c // 2)
        b1 = LANES * (c // 2 + 1)
        w_prev = 0 if first_tile else SEQ_TILE - a0
        return a0, b1, w_prev

    def scores(c):
        a0, b1, w_prev = extents(c)
        b_lo = SEQ_TILE - a0 - w_prev
        for p in range(N_HEAD_GROUPS):
            grp = slice(p * LANES, (p + 1) * LANES)
            qp = q_ref[0, c * CHUNK:(c + 1) * CHUNK, grp]
            zero = jnp.zeros_like(qp)
            qs = jnp.concatenate([jnp.where(first_head, qp, zero),
                                  jnp.where(first_head, zero, qp)], axis=0)
            s = jnp.dot(qs, kt_ref[0, grp, 0:b1], preferred_element_type=f32)
            if w_prev:
                s = jnp.concatenate(
                    [jnp.dot(qs, pkt_ref[0, grp, a0:SEQ_TILE], preferred_element_type=f32), s],
                    axis=1)
            s = s + bias_ref[0, c % 2, p, :, b_lo:b_lo + w_prev + b1]
            s_scr[c % 2, p, :, 0:w_prev + b1] = s
            m_scr[c % 2, p] = jnp.broadcast_to(jnp.max(s, axis=-1, keepdims=True),
                                               (GROUP_ROWS, LANES))

    def exponentiate(c):
        a0, b1, w_prev = extents(c)
        width = w_prev + b1
        for p in range(N_HEAD_GROUPS):
            m = m_scr[c % 2, p]
            e = jnp.concatenate(
                [jnp.exp2(s_scr[c % 2, p, :, lo:lo + LANES] - m) for lo in range(0, width, LANES)],
                axis=1)
            d_scr[c % 2, p] = jnp.broadcast_to(jnp.sum(e, axis=-1, keepdims=True),
                                               (GROUP_ROWS, LANES))
            p_scr[c % 2, p, :, 0:width] = e.astype(bf16)

    def attend(c):
        a0, b1, w_prev = extents(c)
        width = w_prev + b1
        for p in range(N_HEAD_GROUPS):
            grp = slice(p * LANES, (p + 1) * LANES)
            o = jnp.dot(p_scr[c % 2, p, :, w_prev:width], v_ref[0, 0:b1, grp],
                        preferred_element_type=f32)
            if w_prev:
                o = o + jnp.dot(p_scr[c % 2, p, :, 0:w_prev], pv_ref[0, a0:SEQ_TILE, grp],
                                preferred_element_type=f32)
            o = o / d_scr[c % 2, p]
            out = jnp.where(first_head, o[0:CHUNK], o[CHUNK:GROUP_ROWS])
            y_ref[0, c * CHUNK:(c + 1) * CHUNK, grp] = out.astype(bf16)

    for t in range(CHUNKS_PER_TILE + 2):
        if t < CHUNKS_PER_TILE:
            scores(t)
        if 1 <= t <= CHUNKS_PER_TILE:
            exponentiate(t - 1)
        if t >= 2:
            attend(t - 2)


def _mixer_kernel(q_ref, kt_ref, v_ref, pkt_ref, pv_ref, bias_ref, y_ref, s_scr, m_scr, d_scr, p_scr):
    j = pl.program_id(1)
    attn = functools.partial(_attn_tile, q_ref=q_ref, kt_ref=kt_ref, v_ref=v_ref, pkt_ref=pkt_ref,
                             pv_ref=pv_ref, bias_ref=bias_ref, y_ref=y_ref, s_scr=s_scr,
                             m_scr=m_scr, d_scr=d_scr, p_scr=p_scr)
    pl.when(j == 0)(functools.partial(attn, True))
    pl.when(j > 0)(functools.partial(attn, False))


def _mixer_call(l, q, kt, v, bias):
    batch, seq, _ = q.shape
    rows = lambda back: pl.BlockSpec((1, SEQ_TILE, D_ATTN),
                                     lambda b, j: (b, jnp.maximum(j - back, 0), 0))
    cols = lambda back: pl.BlockSpec((1, D_ATTN, SEQ_TILE),
                                     lambda b, j: (b, 0, jnp.maximum(j - back, 0)))
    return pl.pallas_call(
        _mixer_kernel,
        grid=(batch, seq // SEQ_TILE),
        in_specs=[rows(0), cols(0), rows(0), cols(1), rows(1), _layer_block(l, bias.shape[1:])],
        out_specs=pl.BlockSpec((1, SEQ_TILE, D_ATTN), lambda b, j: (b, j, 0)),
        out_shape=jax.ShapeDtypeStruct((batch, seq, D_ATTN), bf16),
        scratch_shapes=[pltpu.VMEM((2, N_HEAD_GROUPS, GROUP_ROWS, BAND_PAD), f32),
                        pltpu.VMEM((2, N_HEAD_GROUPS, GROUP_ROWS, LANES), f32),
                        pltpu.VMEM((2, N_HEAD_GROUPS, GROUP_ROWS, LANES), f32),
                        pltpu.VMEM((2, N_HEAD_GROUPS, GROUP_ROWS, BAND_PAD), bf16)],
        compiler_params=_params(),
        name="mixer",
    )(q, kt, v, kt, v, bias)


def _out_ffn_kernel(alpha, yp_ref, ya_ref, x_ref, mod_ref, wo_ref, ln1_ref, w1_ref, w2_ref, ln2_ref,
                    o_ref):
    g1 = mod_ref[0, 0, 2:3, :]
    sh2 = mod_ref[0, 0, 3:4, :]
    sc2 = mod_ref[0, 0, 4:5, :]
    g2 = mod_ref[0, 0, 5:6, :]
    chains = [slice(i * ROW_TILE, (i + 1) * ROW_TILE) for i in range(FFN_CHAINS)]
    a = [jnp.dot(yp_ref[0, rows, :], wo_ref[0, 0:D_POOL, :], preferred_element_type=f32)
         + jnp.dot(ya_ref[0, rows, :], wo_ref[0, D_POOL:D_MODEL, :], preferred_element_type=f32)
         for rows in chains]
    x1 = [_layer_norm(alpha * x_ref[0, rows, :] + (1.0 + g1) * a_i,
                      ln1_ref[0, 0:1, :], ln1_ref[0, 1:2, :]) for rows, a_i in zip(chains, a)]
    h = [(x1_i * (1.0 + sc2) + sh2).astype(bf16) for x1_i in x1]
    acc = [None] * FFN_CHAINS
    for s in range(D_FF // FF_COLS):
        cols = slice(s * FF_COLS, (s + 1) * FF_COLS)
        for i in range(FFN_CHAINS):
            f = jnp.dot(h[i], w1_ref[0, :, cols], preferred_element_type=f32)
            f = jnp.square(jnp.maximum(f, 0.0)).astype(bf16)
            part = jnp.dot(f, w2_ref[0, cols, :], preferred_element_type=f32)
            acc[i] = part if acc[i] is None else acc[i] + part
    for i, rows in enumerate(chains):
        o_ref[0, rows, :] = _layer_norm(alpha * x1[i] + (1.0 + g2) * acc[i],
                                        ln2_ref[0, 0:1, :], ln2_ref[0, 1:2, :])


def _out_ffn_call(l, alpha, y_pool, y_attn, x, mod, w_out, ln1, w_ff1, w_ff2, ln2):
    batch, seq, d = x.shape
    tile = lambda b, i: (b, i, 0)
    rows = FFN_CHAINS * ROW_TILE
    return pl.pallas_call(
        functools.partial(_out_ffn_kernel, alpha),
        grid=(batch, seq // rows),
        in_specs=[
            pl.BlockSpec((1, rows, D_POOL), tile),
            pl.BlockSpec((1, rows, D_ATTN), tile),
            pl.BlockSpec((1, rows, d), tile),
            pl.BlockSpec((1, 1, N_MOD, d), lambda b, i: (l, b, 0, 0)),
            _layer_block(l, (d, d)),
            _layer_block(l, (2, d)),
            _layer_block(l, (d, D_FF)),
            _layer_block(l, (D_FF, d)),
            _layer_block(l, (2, d)),
        ],
        out_specs=pl.BlockSpec((1, rows, d), tile),
        out_shape=jax.ShapeDtypeStruct((batch, seq, d), f32),
        compiler_params=_params(),
        name="out_ffn",
    )(y_pool, y_attn, x, mod, w_out, ln1, w_ff1, w_ff2, ln2)


def _band_bias_tables(rel_bias):
    tab = rel_bias.astype(f32) * LOG2E
    lead = tab.shape[:-1]
    far = tab[..., -1:]
    n_near = REL_MAX + CHUNK
    period = n_near + CHUNK
    ring = jnp.concatenate([tab[..., ::-1], jnp.broadcast_to(far, (*lead, period - N_REL))], -1)
    near = jnp.tile(ring, (1, 1, CHUNK))[..., :CHUNK * (period - 1)]
    near = near.reshape(*lead, CHUNK, period - 1)[..., :n_near]
    clipped = jnp.broadcast_to(far[..., None], (*lead, CHUNK, BAND_LEN - n_near))
    band = jnp.concatenate([clipped, near], axis=-1)
    masked = jnp.full((*lead, CHUNK, CHUNK), MASK_VALUE, f32)
    tables = jnp.stack([jnp.concatenate([band, masked], -1),
                        jnp.concatenate([masked, band], -1)], axis=1)
    return tables.reshape(lead[0], 2, N_HEAD_GROUPS, GROUP_ROWS, BAND_PAD)


def _pair_block_diagonal(w_pool):
    depth, groups, n, _ = w_pool.shape
    w = w_pool.reshape(depth, groups // 2, 2, n, n)
    zero = jnp.zeros_like(w[:, :, 0])
    top = jnp.concatenate([w[:, :, 0], zero], axis=-1)
    bottom = jnp.concatenate([zero, w[:, :, 1]], axis=-1)
    return jnp.concatenate([top, bottom], axis=-2)


def kernel(x, c, w_ada, b_ada, w_in, w_pool, pool_scale, rel_bias, w_out, ln1_g, ln1_b,
           w_ff1, w_ff2, ln2_g, ln2_b):
    depth = w_in.shape[0]
    batch = x.shape[0]
    alpha = (2.0 * depth) ** 0.25
    mod = _ada_call(c, w_ada, b_ada).reshape(depth, batch, N_MOD, D_MODEL)
    bias = _band_bias_tables(rel_bias)
    w_in, w_pool, w_out, w_ff1, w_ff2 = (w.astype(bf16) for w in (w_in, w_pool, w_out, w_ff1, w_ff2))
    w_pool = _pair_block_diagonal(w_pool)
    w_kt = jnp.swapaxes(w_in[:, :, D_POOL + D_ATTN:D_POOL + 2 * D_ATTN], 1, 2)
    pool_scale = pool_scale.reshape(depth, 1, D_POOL)
    ln1 = jnp.stack([ln1_g, ln1_b], axis=1)
    ln2 = jnp.stack([ln2_g, ln2_b], axis=1)
    for l in range(depth):
        y_pool, q, kt, v = _in_proj_call(l, x, mod, w_in, w_kt, w_pool, pool_scale)
        y_attn = _mixer_call(l, q, kt, v, bias)
        x = _out_ffn_call(l, alpha, y_pool, y_attn, x, mod, w_out, ln1, w_ff1, w_ff2, ln2)
    return x
```

```python
import functools
import math

import jax
import jax.numpy as jnp
from jax import lax
from jax.experimental import pallas as pl
from jax.experimental.pallas import tpu as pltpu

D_MODEL = 1024
CHUNK = 64
D_POOL = D_MODEL // 2
POOL_WINDOWS = (2, 4, 8, 16)
POOL_GROUP_DIM = D_POOL // len(POOL_WINDOWS)
D_ATTN = D_MODEL - D_POOL
N_HEADS = 8
HEAD_DIM = D_ATTN // N_HEADS
LEFT_CHUNKS = 8
BAND = LEFT_CHUNKS + 1
REL_MAX = 128
REL_MIN = CHUNK - 1
N_REL = REL_MIN + REL_MAX + 1
D_FF = 4 * D_MODEL
D_IN = D_POOL + 3 * D_ATTN
N_MOD = 6
LN_EPS = 1e-5
MASK_VALUE = -1e30
LOG2E = math.log2(math.e)

LANES = 128
BAND_LEN = BAND * CHUNK
BAND_PAD = BAND_LEN + CHUNK
HEADS_PER_GROUP = LANES // HEAD_DIM
N_HEAD_GROUPS = N_HEADS // HEADS_PER_GROUP
GROUP_ROWS = HEADS_PER_GROUP * CHUNK
MAX_WINDOW = max(POOL_WINDOWS)

SEQ_TILE = LEFT_CHUNKS * CHUNK
CHUNKS_PER_TILE = SEQ_TILE // CHUNK
ROW_TILE = 512
FFN_CHAINS = 2
IN_ROWS = 1024
ADA_COLS = 1536
FF_COLS = 1024

VMEM_LIMIT = 56 * 1024 * 1024

f32 = jnp.float32
bf16 = jnp.bfloat16


def _layer_block(l, shape):
    return pl.BlockSpec((1, *shape), lambda *_: (l,) + (0,) * len(shape),
                        pipeline_mode=pl.Buffered(1))


def _params():
    return pltpu.CompilerParams(dimension_semantics=("arbitrary", "arbitrary"),
                                vmem_limit_bytes=VMEM_LIMIT)


def _layer_norm(z, g, b):
    mu = jnp.mean(z, axis=-1, keepdims=True)
    zc = z - mu
    var = jnp.mean(zc * zc, axis=-1, keepdims=True)
    return zc * lax.rsqrt(var + LN_EPS) * g + b


def _ada_kernel(c_ref, w_ref, b_ref, o_ref):
    c = c_ref[...]
    c_act = c / (1.0 + jnp.exp(-c))
    o_ref[0] = jnp.dot(c_act, w_ref[0], preferred_element_type=f32) + b_ref[0]


def _ada_call(c, w_ada, b_ada):
    depth, d, n = w_ada.shape
    batch = c.shape[0]
    return pl.pallas_call(
        _ada_kernel,
        grid=(depth, n // ADA_COLS),
        in_specs=[
            pl.BlockSpec((batch, d), lambda l, j: (0, 0)),
            pl.BlockSpec((1, d, ADA_COLS), lambda l, j: (l, 0, j)),
            pl.BlockSpec((1, 1, ADA_COLS), lambda l, j: (l, 0, j)),
        ],
        out_specs=pl.BlockSpec((1, batch, ADA_COLS), lambda l, j: (l, 0, j)),
        out_shape=jax.ShapeDtypeStruct((depth, batch, n), f32),
        compiler_params=_params(),
        name="ada_mod",
    )(c, w_ada, b_ada.reshape(depth, 1, n))


def _pool_rows(first_frame, u, halo, wp_ref, ps_ref, y_ref, rows):
    t_head = first_frame + lax.broadcasted_iota(jnp.int32, (MAX_WINDOW, POOL_GROUP_DIM), 0)
    pooled_groups = []
    for g, w in enumerate(POOL_WINDOWS):
        cols = slice(g * POOL_GROUP_DIM, (g + 1) * POOL_GROUP_DIM)
        tok = u[:, cols]
        win = jnp.concatenate([halo[:, cols], tok], axis=0)
        span = 1
        while span < w:
            win = win + pltpu.roll(win, span, axis=0)
            span *= 2
        win = win[MAX_WINDOW:]
        head = win[:MAX_WINDOW] / jnp.minimum(t_head + 1, w).astype(f32) - tok[:MAX_WINDOW]
        rest = win[MAX_WINDOW:] * (1.0 / w) - tok[MAX_WINDOW:]
        pooled_groups.append(jnp.concatenate([head, rest], axis=0).astype(bf16))
    for k in range(len(POOL_WINDOWS) // 2):
        cols = slice(2 * k * POOL_GROUP_DIM, 2 * (k + 1) * POOL_GROUP_DIM)
        pooled = jnp.concatenate(pooled_groups[2 * k:2 * k + 2], axis=1)
        yp = jnp.dot(pooled, wp_ref[0, k], preferred_element_type=f32)
        y_ref[0, rows, cols] = (yp * ps_ref[0, :, cols]).astype(bf16)


def _in_proj_kernel(x_ref, mod_ref, w_ref, wp_ref, ps_ref, yp_ref, q_ref, kt_ref, v_ref, halo_scr,
                    w_scr, wkt_scr):
    i = pl.program_id(1)
    k_lo, v_lo = D_POOL + D_ATTN, D_POOL + 2 * D_ATTN

    @pl.when((pl.program_id(0) == 0) & (i == 0))
    def _():
        for lo in range(0, k_lo, D_ATTN):
            w_scr[:, lo:lo + D_ATTN] = w_ref[0, :, lo:lo + D_ATTN].astype(bf16)
        w_scr[:, k_lo:v_lo] = w_ref[0, :, v_lo:D_IN].astype(bf16)
        wkt_scr[...] = w_ref[0, :, k_lo:v_lo].T.astype(bf16)

    @pl.when(i == 0)
    def _():
        halo_scr[...] = jnp.zeros((MAX_WINDOW, D_POOL), f32)

    sh1 = mod_ref[0, 0, 0:1, :]
    sc1 = mod_ref[0, 0, 1:2, :]
    halo = halo_scr[...]
    for r in range(0, IN_ROWS, ROW_TILE):
        rows = slice(r, r + ROW_TILE)
        h = (x_ref[0, rows, :] * (1.0 + sc1) + sh1).astype(bf16)
        u = jnp.dot(h, w_scr[:, 0:D_POOL], preferred_element_type=f32)
        _pool_rows(i * IN_ROWS + r, u, halo, wp_ref, ps_ref, yp_ref, rows)
        halo = u[ROW_TILE - MAX_WINDOW:]
        q = jnp.dot(h, w_scr[:, D_POOL:k_lo], preferred_element_type=f32)
        q_ref[0, rows, :] = (q * (HEAD_DIM ** -0.5 * LOG2E)).astype(bf16)
        kt = lax.dot_general(wkt_scr[...], h, (((1,), (1,)), ((), ())), preferred_element_type=f32)
        kt_ref[0, :, rows] = kt.astype(bf16)
        v = jnp.dot(h, w_scr[:, k_lo:v_lo], preferred_element_type=f32)
        v_ref[0, rows, :] = v.astype(bf16)
    halo_scr[...] = halo


def _in_proj_call(l, x, mod, w_in, w_pool, pool_scale):
    batch, seq, d = x.shape
    return pl.pallas_call(
        _in_proj_kernel,
        grid=(batch, seq // IN_ROWS),
        in_specs=[
            pl.BlockSpec((1, IN_ROWS, d), lambda b, i: (b, i, 0)),
            pl.BlockSpec((1, 1, N_MOD, d), lambda b, i: (l, b, 0, 0)),
            _layer_block(l, (d, D_IN)),
            _layer_block(l, w_pool.shape[1:]),
            _layer_block(l, pool_scale.shape[1:]),
        ],
        out_specs=[
            pl.BlockSpec((1, IN_ROWS, D_POOL), lambda b, i: (b, i, 0)),
            pl.BlockSpec((1, IN_ROWS, D_ATTN), lambda b, i: (b, i, 0)),
            pl.BlockSpec((1, D_ATTN, IN_ROWS), lambda b, i: (b, 0, i)),
            pl.BlockSpec((1, IN_ROWS, D_ATTN), lambda b, i: (b, i, 0)),
        ],
        out_shape=[
            jax.ShapeDtypeStruct((batch, seq, D_POOL), bf16),
            jax.ShapeDtypeStruct((batch, seq, D_ATTN), bf16),
            jax.ShapeDtypeStruct((batch, D_ATTN, seq), bf16),
            jax.ShapeDtypeStruct((batch, seq, D_ATTN), bf16),
        ],
        scratch_shapes=[pltpu.VMEM((MAX_WINDOW, D_POOL), f32),
                        pltpu.VMEM((d, D_IN - D_ATTN), bf16),
                        pltpu.VMEM((D_ATTN, d), bf16)],
        compiler_params=_params(),
        name="in_proj",
    )(x, mod, w_in, w_pool, pool_scale)


def _attn_tile(first_tile, q_ref, kt_ref, v_ref, pkt_ref, pv_ref, bias_ref, y_ref, s_scr, m_scr,
               d_scr, p_scr):
    lane = lax.broadcasted_iota(jnp.int32, (CHUNK, LANES), 1)
    first_head = lane < HEAD_DIM

    def extents(c):
        a0 = LANES * (c // 2)
        b1 = LANES * (c // 2 + 1)
        w_prev = 0 if first_tile else SEQ_TILE - a0
        return a0, b1, w_prev

    def scores(c):
        a0, b1, w_prev = extents(c)
        b_lo = SEQ_TILE - a0 - w_prev
        for p in range(N_HEAD_GROUPS):
            grp = slice(p * LANES, (p + 1) * LANES)
            qp = q_ref[0, c * CHUNK:(c + 1) * CHUNK, grp]
            zero = jnp.zeros_like(qp)
            qs = jnp.concatenate([jnp.where(first_head, qp, zero),
                                  jnp.where(first_head, zero, qp)], axis=0)
            s = jnp.dot(qs, kt_ref[0, grp, 0:b1], preferred_element_type=f32)
            if w_prev:
                s = jnp.concatenate(
                    [jnp.dot(qs, pkt_ref[0, grp, a0:SEQ_TILE], preferred_element_type=f32), s],
                    axis=1)
            s = s + bias_ref[0, c % 2, p, :, b_lo:b_lo + w_prev + b1]
            s_scr[c % 2, p, :, 0:w_prev + b1] = s
            m_scr[c % 2, p] = jnp.broadcast_to(jnp.max(s, axis=-1, keepdims=True),
                                               (GROUP_ROWS, LANES))

    def exponentiate(c):
        a0, b1, w_prev = extents(c)
        width = w_prev + b1
        for p in range(N_HEAD_GROUPS):
            m = m_scr[c % 2, p]
            e = jnp.concatenate(
                [jnp.exp2(s_scr[c % 2, p, :, lo:lo + LANES] - m) for lo in range(0, width, LANES)],
                axis=1)
            d_scr[c % 2, p] = jnp.broadcast_to(jnp.sum(e, axis=-1, keepdims=True),
                                               (GROUP_ROWS, LANES))
            p_scr[c % 2, p, :, 0:width] = e.astype(bf16)

    def attend(c):
        a0, b1, w_prev = extents(c)
        width = w_prev + b1
        for p in range(N_HEAD_GROUPS):
            grp = slice(p * LANES, (p + 1) * LANES)
            o = jnp.dot(p_scr[c % 2, p, :, w_prev:width], v_ref[0, 0:b1, grp],
                        preferred_element_type=f32)
            if w_prev:
                o = o + jnp.dot(p_scr[c % 2, p, :, 0:w_prev], pv_ref[0, a0:SEQ_TILE, grp],
                                preferred_element_type=f32)
            o = o / d_scr[c % 2, p]
            out = jnp.where(first_head, o[0:CHUNK], o[CHUNK:GROUP_ROWS])
            y_ref[0, c * CHUNK:(c + 1) * CHUNK, grp] = out.astype(bf16)

    for t in range(CHUNKS_PER_TILE + 2):
        if t < CHUNKS_PER_TILE:
            scores(t)
        if 1 <= t <= CHUNKS_PER_TILE:
            exponentiate(t - 1)
        if t >= 2:
            attend(t - 2)


def _mixer_kernel(q_ref, kt_ref, v_ref, pkt_ref, pv_ref, bias_ref, y_ref, s_scr, m_scr, d_scr, p_scr):
    j = pl.program_id(1)
    attn = functools.partial(_attn_tile, q_ref=q_ref, kt_ref=kt_ref, v_ref=v_ref, pkt_ref=pkt_ref,
                             pv_ref=pv_ref, bias_ref=bias_ref, y_ref=y_ref, s_scr=s_scr,
                             m_scr=m_scr, d_scr=d_scr, p_scr=p_scr)
    pl.when(j == 0)(functools.partial(attn, True))
    pl.when(j > 0)(functools.partial(attn, False))


def _mixer_call(l, q, kt, v, bias):
    batch, seq, _ = q.shape
    rows = lambda back: pl.BlockSpec((1, SEQ_TILE, D_ATTN),
                                     lambda b, j: (b, jnp.maximum(j - back, 0), 0))
    cols = lambda back: pl.BlockSpec((1, D_ATTN, SEQ_TILE),
                                     lambda b, j: (b, 0, jnp.maximum(j - back, 0)))
    return pl.pallas_call(
        _mixer_kernel,
        grid=(batch, seq // SEQ_TILE),
        in_specs=[rows(0), cols(0), rows(0), cols(1), rows(1), _layer_block(l, bias.shape[1:])],
        out_specs=pl.BlockSpec((1, SEQ_TILE, D_ATTN), lambda b, j: (b, j, 0)),
        out_shape=jax.ShapeDtypeStruct((batch, seq, D_ATTN), bf16),
        scratch_shapes=[pltpu.VMEM((2, N_HEAD_GROUPS, GROUP_ROWS, BAND_PAD), f32),
                        pltpu.VMEM((2, N_HEAD_GROUPS, GROUP_ROWS, LANES), f32),
                        pltpu.VMEM((2, N_HEAD_GROUPS, GROUP_ROWS, LANES), f32),
                        pltpu.VMEM((2, N_HEAD_GROUPS, GROUP_ROWS, BAND_PAD), bf16)],
        compiler_params=_params(),
        name="mixer",
    )(q, kt, v, kt, v, bias)


def _out_ffn_kernel(alpha, yp_ref, ya_ref, x_ref, mod_ref, wo_ref, ln1_ref, w1_ref, w2_ref, ln2_ref,
                    o_ref):
    g1 = mod_ref[0, 0, 2:3, :]
    sh2 = mod_ref[0, 0, 3:4, :]
    sc2 = mod_ref[0, 0, 4:5, :]
    g2 = mod_ref[0, 0, 5:6, :]
    chains = [slice(i * ROW_TILE, (i + 1) * ROW_TILE) for i in range(FFN_CHAINS)]
    a = [jnp.dot(yp_ref[0, rows, :], wo_ref[0, 0:D_POOL, :], preferred_element_type=f32)
         + jnp.dot(ya_ref[0, rows, :], wo_ref[0, D_POOL:D_MODEL, :], preferred_element_type=f32)
         for rows in chains]
    x1 = [_layer_norm(alpha * x_ref[0, rows, :] + (1.0 + g1) * a_i,
                      ln1_ref[0, 0:1, :], ln1_ref[0, 1:2, :]) for rows, a_i in zip(chains, a)]
    h = [(x1_i * (1.0 + sc2) + sh2).astype(bf16) for x1_i in x1]
    acc = [None] * FFN_CHAINS
    for s in range(D_FF // FF_COLS):
        cols = slice(s * FF_COLS, (s + 1) * FF_COLS)
        for i in range(FFN_CHAINS):
            f = jnp.dot(h[i], w1_ref[0, :, cols], preferred_element_type=f32)
            f = jnp.square(jnp.maximum(f, 0.0)).astype(bf16)
            part = jnp.dot(f, w2_ref[0, cols, :], preferred_element_type=f32)
            acc[i] = part if acc[i] is None else acc[i] + part
    for i, rows in enumerate(chains):
        o_ref[0, rows, :] = _layer_norm(alpha * x1[i] + (1.0 + g2) * acc[i],
                                        ln2_ref[0, 0:1, :], ln2_ref[0, 1:2, :])


def _out_ffn_call(l, alpha, y_pool, y_attn, x, mod, w_out, ln1, w_ff1, w_ff2, ln2):
    batch, seq, d = x.shape
    tile = lambda b, i: (b, i, 0)
    rows = FFN_CHAINS * ROW_TILE
    return pl.pallas_call(
        functools.partial(_out_ffn_kernel, alpha),
        grid=(batch, seq // rows),
        in_specs=[
            pl.BlockSpec((1, rows, D_POOL), tile),
            pl.BlockSpec((1, rows, D_ATTN), tile),
            pl.BlockSpec((1, rows, d), tile),
            pl.BlockSpec((1, 1, N_MOD, d), lambda b, i: (l, b, 0, 0)),
            _layer_block(l, (d, d)),
            _layer_block(l, (2, d)),
            _layer_block(l, (d, D_FF)),
            _layer_block(l, (D_FF, d)),
            _layer_block(l, (2, d)),
        ],
        out_specs=pl.BlockSpec((1, rows, d), tile),
        out_shape=jax.ShapeDtypeStruct((batch, seq, d), f32),
        compiler_params=_params(),
        name="out_ffn",
    )(y_pool, y_attn, x, mod, w_out, ln1, w_ff1, w_ff2, ln2)


def _band_bias_tables(rel_bias):
    tab = rel_bias.astype(f32) * LOG2E
    lead = tab.shape[:-1]
    far = tab[..., -1:]
    n_near = REL_MAX + CHUNK
    period = n_near + CHUNK
    ring = jnp.concatenate([tab[..., ::-1], jnp.broadcast_to(far, (*lead, period - N_REL))], -1)
    near = jnp.tile(ring, (1, 1, CHUNK))[..., :CHUNK * (period - 1)]
    near = near.reshape(*lead, CHUNK, period - 1)[..., :n_near]
    clipped = jnp.broadcast_to(far[..., None], (*lead, CHUNK, BAND_LEN - n_near))
    band = jnp.concatenate([clipped, near], axis=-1)
    masked = jnp.full((*lead, CHUNK, CHUNK), MASK_VALUE, f32)
    tables = jnp.stack([jnp.concatenate([band, masked], -1),
                        jnp.concatenate([masked, band], -1)], axis=1)
    return tables.reshape(lead[0], 2, N_HEAD_GROUPS, GROUP_ROWS, BAND_PAD)


def _pair_block_diagonal(w_pool):
    depth, groups, n, _ = w_pool.shape
    w = w_pool.reshape(depth, groups // 2, 2, n, n)
    zero = jnp.zeros_like(w[:, :, 0])
    top = jnp.concatenate([w[:, :, 0], zero], axis=-1)
    bottom = jnp.concatenate([zero, w[:, :, 1]], axis=-1)
    return jnp.concatenate([top, bottom], axis=-2)


def kernel(x, c, w_ada, b_ada, w_in, w_pool, pool_scale, rel_bias, w_out, ln1_g, ln1_b,
           w_ff1, w_ff2, ln2_g, ln2_b):
    depth = w_in.shape[0]
    batch = x.shape[0]
    alpha = (2.0 * depth) ** 0.25
    mod = _ada_call(c, w_ada, b_ada).reshape(depth, batch, N_MOD, D_MODEL)
    bias = _band_bias_tables(rel_bias)
    w_pool, w_out, w_ff1, w_ff2 = (w.astype(bf16) for w in (w_pool, w_out, w_ff1, w_ff2))
    w_pool = _pair_block_diagonal(w_pool)
    pool_scale = pool_scale.reshape(depth, 1, D_POOL)
    ln1 = jnp.stack([ln1_g, ln1_b], axis=1)
    ln2 = jnp.stack([ln2_g, ln2_b], axis=1)
    for l in range(depth):
        y_pool, q, kt, v = _in_proj_call(l, x, mod, w_in, w_pool, pool_scale)
        y_attn = _mixer_call(l, q, kt, v, bias)
        x = _out_ffn_call(l, alpha, y_pool, y_attn, x, mod, w_out, ln1, w_ff1, w_ff2, ln2)
    return x
```

```python
import functools
import math

import jax
import jax.numpy as jnp
from jax import lax
from jax.experimental import pallas as pl
from jax.experimental.pallas import tpu as pltpu

D_MODEL = 1024
CHUNK = 64
D_POOL = D_MODEL // 2
POOL_WINDOWS = (2, 4, 8, 16)
POOL_GROUP_DIM = D_POOL // len(POOL_WINDOWS)
D_ATTN = D_MODEL - D_POOL
N_HEADS = 8
HEAD_DIM = D_ATTN // N_HEADS
LEFT_CHUNKS = 8
BAND = LEFT_CHUNKS + 1
REL_MAX = 128
REL_MIN = CHUNK - 1
N_REL = REL_MIN + REL_MAX + 1
D_FF = 4 * D_MODEL
D_IN = D_POOL + 3 * D_ATTN
N_MOD = 6
LN_EPS = 1e-5
MASK_VALUE = -1e30
LOG2E = math.log2(math.e)

LANES = 128
BAND_LEN = BAND * CHUNK
BAND_PAD = BAND_LEN + CHUNK
HEADS_PER_GROUP = LANES // HEAD_DIM
N_HEAD_GROUPS = N_HEADS // HEADS_PER_GROUP
GROUP_ROWS = HEADS_PER_GROUP * CHUNK
MAX_WINDOW = max(POOL_WINDOWS)

SEQ_TILE = LEFT_CHUNKS * CHUNK
CHUNKS_PER_TILE = SEQ_TILE // CHUNK
ROW_TILE = 512
FFN_CHAINS = 2
IN_ROWS = 1024
ADA_COLS = 1536
FF_COLS = 1024

VMEM_LIMIT = 56 * 1024 * 1024

f32 = jnp.float32
bf16 = jnp.bfloat16


def _layer_block(l, shape):
    return pl.BlockSpec((1, *shape), lambda *_: (l,) + (0,) * len(shape),
                        pipeline_mode=pl.Buffered(1))


def _resident(shape):
    return pl.BlockSpec(shape, lambda *_: (0,) * len(shape), pipeline_mode=pl.Buffered(1))


def _params():
    return pltpu.CompilerParams(dimension_semantics=("arbitrary", "arbitrary"),
                                vmem_limit_bytes=VMEM_LIMIT)


def _layer_norm(z, g, b):
    mu = jnp.mean(z, axis=-1, keepdims=True)
    zc = z - mu
    var = jnp.mean(zc * zc, axis=-1, keepdims=True)
    return zc * lax.rsqrt(var + LN_EPS) * g + b


def _ada_kernel(c_ref, w_ref, b_ref, o_ref):
    c = c_ref[...]
    c_act = c / (1.0 + jnp.exp(-c))
    o_ref[0] = jnp.dot(c_act, w_ref[0], preferred_element_type=f32) + b_ref[0]


def _ada_call(c, w_ada, b_ada):
    depth, d, n = w_ada.shape
    batch = c.shape[0]
    return pl.pallas_call(
        _ada_kernel,
        grid=(depth, n // ADA_COLS),
        in_specs=[
            pl.BlockSpec((batch, d), lambda l, j: (0, 0)),
            pl.BlockSpec((1, d, ADA_COLS), lambda l, j: (l, 0, j)),
            pl.BlockSpec((1, 1, ADA_COLS), lambda l, j: (l, 0, j)),
        ],
        out_specs=pl.BlockSpec((1, batch, ADA_COLS), lambda l, j: (l, 0, j)),
        out_shape=jax.ShapeDtypeStruct((depth, batch, n), f32),
        compiler_params=_params(),
        name="ada_mod",
    )(c, w_ada, b_ada.reshape(depth, 1, n))


def _pool_rows(first_frame, u, halo, wp_ref, ps_ref, y_ref, rows):
    t_head = first_frame + lax.broadcasted_iota(jnp.int32, (MAX_WINDOW, POOL_GROUP_DIM), 0)
    pooled_groups = []
    for g, w in enumerate(POOL_WINDOWS):
        cols = slice(g * POOL_GROUP_DIM, (g + 1) * POOL_GROUP_DIM)
        tok = u[:, cols]
        win = jnp.concatenate([halo[:, cols], tok], axis=0)
        span = 1
        while span < w:
            win = win + pltpu.roll(win, span, axis=0)
            span *= 2
        win = win[MAX_WINDOW:]
        head = win[:MAX_WINDOW] / jnp.minimum(t_head + 1, w).astype(f32) - tok[:MAX_WINDOW]
        rest = win[MAX_WINDOW:] * (1.0 / w) - tok[MAX_WINDOW:]
        pooled_groups.append(jnp.concatenate([head, rest], axis=0).astype(bf16))
    for k in range(len(POOL_WINDOWS) // 2):
        cols = slice(2 * k * POOL_GROUP_DIM, 2 * (k + 1) * POOL_GROUP_DIM)
        pooled = jnp.concatenate(pooled_groups[2 * k:2 * k + 2], axis=1)
        yp = jnp.dot(pooled, wp_ref[0, k], preferred_element_type=f32)
        y_ref[0, rows, cols] = (yp * ps_ref[0, :, cols]).astype(bf16)


def _in_proj_kernel(x_ref, mod_ref, w_ref, wp_ref, ps_ref, yp_ref, q_ref, kt_ref, v_ref, halo_scr,
                    w_scr, wkt_scr):
    i = pl.program_id(1)
    k_lo, v_lo = D_POOL + D_ATTN, D_POOL + 2 * D_ATTN

    @pl.when((pl.program_id(0) == 0) & (i == 0))
    def _():
        for lo in range(0, k_lo, D_ATTN):
            w_scr[:, lo:lo + D_ATTN] = w_ref[0, :, lo:lo + D_ATTN].astype(bf16)
        w_scr[:, k_lo:v_lo] = w_ref[0, :, v_lo:D_IN].astype(bf16)
        wkt_scr[...] = w_ref[0, :, k_lo:v_lo].T.astype(bf16)

    @pl.when(i == 0)
    def _():
        halo_scr[...] = jnp.zeros((MAX_WINDOW, D_POOL), f32)

    sh1 = mod_ref[0, 0, 0:1, :]
    sc1 = mod_ref[0, 0, 1:2, :]
    halo = halo_scr[...]
    for r in range(0, IN_ROWS, ROW_TILE):
        rows = slice(r, r + ROW_TILE)
        h = (x_ref[0, rows, :] * (1.0 + sc1) + sh1).astype(bf16)
        u = jnp.dot(h, w_scr[:, 0:D_POOL], preferred_element_type=f32)
        _pool_rows(i * IN_ROWS + r, u, halo, wp_ref, ps_ref, yp_ref, rows)
        halo = u[ROW_TILE - MAX_WINDOW:]
        q = jnp.dot(h, w_scr[:, D_POOL:k_lo], preferred_element_type=f32)
        q_ref[0, rows, :] = (q * (HEAD_DIM ** -0.5 * LOG2E)).astype(bf16)
        kt = lax.dot_general(wkt_scr[...], h, (((1,), (1,)), ((), ())), preferred_element_type=f32)
        kt_ref[0, :, rows] = kt.astype(bf16)
        v = jnp.dot(h, w_scr[:, k_lo:v_lo], preferred_element_type=f32)
        v_ref[0, rows, :] = v.astype(bf16)
    halo_scr[...] = halo


def _in_proj_call(l, x, mod, w_in, w_pool, pool_scale):
    batch, seq, d = x.shape
    return pl.pallas_call(
        _in_proj_kernel,
        grid=(batch, seq // IN_ROWS),
        in_specs=[
            pl.BlockSpec((1, IN_ROWS, d), lambda b, i: (b, i, 0)),
            pl.BlockSpec((1, 1, N_MOD, d), lambda b, i: (l, b, 0, 0)),
            _layer_block(l, (d, D_IN)),
            _layer_block(l, w_pool.shape[1:]),
            _layer_block(l, pool_scale.shape[1:]),
        ],
        out_specs=[
            pl.BlockSpec((1, IN_ROWS, D_POOL), lambda b, i: (b, i, 0)),
            pl.BlockSpec((1, IN_ROWS, D_ATTN), lambda b, i: (b, i, 0)),
            pl.BlockSpec((1, D_ATTN, IN_ROWS), lambda b, i: (b, 0, i)),
            pl.BlockSpec((1, IN_ROWS, D_ATTN), lambda b, i: (b, i, 0)),
        ],
        out_shape=[
            jax.ShapeDtypeStruct((batch, seq, D_POOL), bf16),
            jax.ShapeDtypeStruct((batch, seq, D_ATTN), bf16),
            jax.ShapeDtypeStruct((batch, D_ATTN, seq), bf16),
            jax.ShapeDtypeStruct((batch, seq, D_ATTN), bf16),
        ],
        scratch_shapes=[pltpu.VMEM((MAX_WINDOW, D_POOL), f32),
                        pltpu.VMEM((d, D_IN - D_ATTN), bf16),
                        pltpu.VMEM((D_ATTN, d), bf16)],
        compiler_params=_params(),
        name="in_proj",
    )(x, mod, w_in, w_pool, pool_scale)


def _attn_tile(first_tile, q_ref, kt_ref, v_ref, pkt_ref, pv_ref, bias_ref, y_ref, s_scr, m_scr,
               d_scr, p_scr):
    lane = lax.broadcasted_iota(jnp.int32, (CHUNK, LANES), 1)
    first_head = lane < HEAD_DIM

    def extents(c):
        a0 = LANES * (c // 2)
        b1 = LANES * (c // 2 + 1)
        w_prev = 0 if first_tile else SEQ_TILE - a0
        return a0, b1, w_prev

    def scores(c):
        a0, b1, w_prev = extents(c)
        b_lo = SEQ_TILE - a0 - w_prev
        for p in range(N_HEAD_GROUPS):
            grp = slice(p * LANES, (p + 1) * LANES)
            qp = q_ref[0, c * CHUNK:(c + 1) * CHUNK, grp]
            zero = jnp.zeros_like(qp)
            qs = jnp.concatenate([jnp.where(first_head, qp, zero),
                                  jnp.where(first_head, zero, qp)], axis=0)
            s = jnp.dot(qs, kt_ref[0, grp, 0:b1], preferred_element_type=f32)
            if w_prev:
                s = jnp.concatenate(
                    [jnp.dot(qs, pkt_ref[0, grp, a0:SEQ_TILE], preferred_element_type=f32), s],
                    axis=1)
            s = s + bias_ref[0, c % 2, p, :, b_lo:b_lo + w_prev + b1]
            s_scr[c % 2, p, :, 0:w_prev + b1] = s
            m_scr[c % 2, p] = jnp.broadcast_to(jnp.max(s, axis=-1, keepdims=True),
                                               (GROUP_ROWS, LANES))

    def exponentiate(c):
        a0, b1, w_prev = extents(c)
        width = w_prev + b1
        for p in range(N_HEAD_GROUPS):
            m = m_scr[c % 2, p]
            e = jnp.concatenate(
                [jnp.exp2(s_scr[c % 2, p, :, lo:lo + LANES] - m) for lo in range(0, width, LANES)],
                axis=1)
            d_scr[c % 2, p] = jnp.broadcast_to(jnp.sum(e, axis=-1, keepdims=True),
                                               (GROUP_ROWS, LANES))
            p_scr[c % 2, p, :, 0:width] = e.astype(bf16)

    def attend(c):
        a0, b1, w_prev = extents(c)
        width = w_prev + b1
        for p in range(N_HEAD_GROUPS):
            grp = slice(p * LANES, (p + 1) * LANES)
            o = jnp.dot(p_scr[c % 2, p, :, w_prev:width], v_ref[0, 0:b1, grp],
                        preferred_element_type=f32)
            if w_prev:
                o = o + jnp.dot(p_scr[c % 2, p, :, 0:w_prev], pv_ref[0, a0:SEQ_TILE, grp],
                                preferred_element_type=f32)
            o = o / d_scr[c % 2, p]
            out = jnp.where(first_head, o[0:CHUNK], o[CHUNK:GROUP_ROWS])
            y_ref[0, c * CHUNK:(c + 1) * CHUNK, grp] = out.astype(bf16)

    for t in range(CHUNKS_PER_TILE + 2):
        if t < CHUNKS_PER_TILE:
            scores(t)
        if 1 <= t <= CHUNKS_PER_TILE:
            exponentiate(t - 1)
        if t >= 2:
            attend(t - 2)


def _mixer_kernel(q_ref, kt_ref, v_ref, pkt_ref, pv_ref, bias_ref, wo_ref, w1_ref, w2_ref,
                  y_ref, wo_bf_ref, w1_bf_ref, w2_bf_ref, s_scr, m_scr, d_scr, p_scr):
    j = pl.program_id(1)
    for src, dst in ((wo_ref, wo_bf_ref), (w1_ref, w1_bf_ref), (w2_ref, w2_bf_ref)):
        dst[...] = src[0].astype(bf16)
    attn = functools.partial(_attn_tile, q_ref=q_ref, kt_ref=kt_ref, v_ref=v_ref, pkt_ref=pkt_ref,
                             pv_ref=pv_ref, bias_ref=bias_ref, y_ref=y_ref, s_scr=s_scr,
                             m_scr=m_scr, d_scr=d_scr, p_scr=p_scr)
    pl.when(j == 0)(functools.partial(attn, True))
    pl.when(j > 0)(functools.partial(attn, False))


def _mixer_call(l, q, kt, v, bias, w_out, w_ff1, w_ff2):
    batch, seq, _ = q.shape
    steps = batch * (seq // SEQ_TILE)
    step = lambda b, j: b * (seq // SEQ_TILE) + j
    weights = (w_out, w_ff1, w_ff2)
    slab_rows = [w.shape[1] // steps for w in weights]
    rows = lambda back: pl.BlockSpec((1, SEQ_TILE, D_ATTN),
                                     lambda b, j: (b, jnp.maximum(j - back, 0), 0))
    cols = lambda back: pl.BlockSpec((1, D_ATTN, SEQ_TILE),
                                     lambda b, j: (b, 0, jnp.maximum(j - back, 0)))
    return pl.pallas_call(
        _mixer_kernel,
        grid=(batch, seq // SEQ_TILE),
        in_specs=[rows(0), cols(0), rows(0), cols(1), rows(1), _layer_block(l, bias.shape[1:])]
        + [pl.BlockSpec((1, r, w.shape[2]), lambda b, j: (l, step(b, j), 0))
           for w, r in zip(weights, slab_rows)],
        out_specs=[pl.BlockSpec((1, SEQ_TILE, D_ATTN), lambda b, j: (b, j, 0))]
        + [pl.BlockSpec((r, w.shape[2]), lambda b, j: (step(b, j), 0))
           for w, r in zip(weights, slab_rows)],
        out_shape=[jax.ShapeDtypeStruct((batch, seq, D_ATTN), bf16)]
        + [jax.ShapeDtypeStruct(w.shape[1:], bf16) for w in weights],
        scratch_shapes=[pltpu.VMEM((2, N_HEAD_GROUPS, GROUP_ROWS, BAND_PAD), f32),
                        pltpu.VMEM((2, N_HEAD_GROUPS, GROUP_ROWS, LANES), f32),
                        pltpu.VMEM((2, N_HEAD_GROUPS, GROUP_ROWS, LANES), f32),
                        pltpu.VMEM((2, N_HEAD_GROUPS, GROUP_ROWS, BAND_PAD), bf16)],
        compiler_params=_params(),
        name="mixer",
    )(q, kt, v, kt, v, bias, *weights)


def _out_ffn_kernel(alpha, yp_ref, ya_ref, x_ref, mod_ref, wo_ref, ln1_ref, w1_ref, w2_ref, ln2_ref,
                    o_ref):
    g1 = mod_ref[0, 0, 2:3, :]
    sh2 = mod_ref[0, 0, 3:4, :]
    sc2 = mod_ref[0, 0, 4:5, :]
    g2 = mod_ref[0, 0, 5:6, :]
    chains = [slice(i * ROW_TILE, (i + 1) * ROW_TILE) for i in range(FFN_CHAINS)]
    a = [jnp.dot(yp_ref[0, rows, :], wo_ref[0:D_POOL, :], preferred_element_type=f32)
         + jnp.dot(ya_ref[0, rows, :], wo_ref[D_POOL:D_MODEL, :], preferred_element_type=f32)
         for rows in chains]
    x1 = [_layer_norm(alpha * x_ref[0, rows, :] + (1.0 + g1) * a_i,
                      ln1_ref[0, 0:1, :], ln1_ref[0, 1:2, :]) for rows, a_i in zip(chains, a)]
    h = [(x1_i * (1.0 + sc2) + sh2).astype(bf16) for x1_i in x1]
    acc = [None] * FFN_CHAINS
    for s in range(D_FF // FF_COLS):
        cols = slice(s * FF_COLS, (s + 1) * FF_COLS)
        for i in range(FFN_CHAINS):
            f = jnp.dot(h[i], w1_ref[:, cols], preferred_element_type=f32)
            f = jnp.square(jnp.maximum(f, 0.0)).astype(bf16)
            part = jnp.dot(f, w2_ref[cols, :], preferred_element_type=f32)
            acc[i] = part if acc[i] is None else acc[i] + part
    for i, rows in enumerate(chains):
        o_ref[0, rows, :] = _layer_norm(alpha * x1[i] + (1.0 + g2) * acc[i],
                                        ln2_ref[0, 0:1, :], ln2_ref[0, 1:2, :])


def _out_ffn_call(l, alpha, y_pool, y_attn, x, mod, w_out, ln1, w_ff1, w_ff2, ln2):
    batch, seq, d = x.shape
    tile = lambda b, i: (b, i, 0)
    rows = FFN_CHAINS * ROW_TILE
    return pl.pallas_call(
        functools.partial(_out_ffn_kernel, alpha),
        grid=(batch, seq // rows),
        in_specs=[
            pl.BlockSpec((1, rows, D_POOL), tile),
            pl.BlockSpec((1, rows, D_ATTN), tile),
            pl.BlockSpec((1, rows, d), tile),
            pl.BlockSpec((1, 1, N_MOD, d), lambda b, i: (l, b, 0, 0)),
            _resident((d, d)),
            _layer_block(l, (2, d)),
            _resident((d, D_FF)),
            _resident((D_FF, d)),
            _layer_block(l, (2, d)),
        ],
        out_specs=pl.BlockSpec((1, rows, d), tile),
        out_shape=jax.ShapeDtypeStruct((batch, seq, d), f32),
        compiler_params=_params(),
        name="out_ffn",
    )(y_pool, y_attn, x, mod, w_out, ln1, w_ff1, w_ff2, ln2)


def _band_bias_tables(rel_bias):
    tab = rel_bias.astype(f32) * LOG2E
    lead = tab.shape[:-1]
    far = tab[..., -1:]
    n_near = REL_MAX + CHUNK
    period = n_near + CHUNK
    ring = jnp.concatenate([tab[..., ::-1], jnp.broadcast_to(far, (*lead, period - N_REL))], -1)
    near = jnp.tile(ring, (1, 1, CHUNK))[..., :CHUNK * (period - 1)]
    near = near.reshape(*lead, CHUNK, period - 1)[..., :n_near]
    clipped = jnp.broadcast_to(far[..., None], (*lead, CHUNK, BAND_LEN - n_near))
    band = jnp.concatenate([clipped, near], axis=-1)
    masked = jnp.full((*lead, CHUNK, CHUNK), MASK_VALUE, f32)
    tables = jnp.stack([jnp.concatenate([band, masked], -1),
                        jnp.concatenate([masked, band], -1)], axis=1)
    return tables.reshape(lead[0], 2, N_HEAD_GROUPS, GROUP_ROWS, BAND_PAD)


def _pair_block_diagonal(w_pool):
    depth, groups, n, _ = w_pool.shape
    w = w_pool.reshape(depth, groups // 2, 2, n, n)
    zero = jnp.zeros_like(w[:, :, 0])
    top = jnp.concatenate([w[:, :, 0], zero], axis=-1)
    bottom = jnp.concatenate([zero, w[:, :, 1]], axis=-1)
    return jnp.concatenate([top, bottom], axis=-2)


def kernel(x, c, w_ada, b_ada, w_in, w_pool, pool_scale, rel_bias, w_out, ln1_g, ln1_b,
           w_ff1, w_ff2, ln2_g, ln2_b):
    depth = w_in.shape[0]
    batch = x.shape[0]
    alpha = (2.0 * depth) ** 0.25
    mod = _ada_call(c, w_ada, b_ada).reshape(depth, batch, N_MOD, D_MODEL)
    bias = _band_bias_tables(rel_bias)
    w_pool = _pair_block_diagonal(w_pool.astype(bf16))
    pool_scale = pool_scale.reshape(depth, 1, D_POOL)
    ln1 = jnp.stack([ln1_g, ln1_b], axis=1)
    ln2 = jnp.stack([ln2_g, ln2_b], axis=1)
    for l in range(depth):
        y_pool, q, kt, v = _in_proj_call(l, x, mod, w_in, w_pool, pool_scale)
        y_attn, wo_bf, w1_bf, w2_bf = _mixer_call(l, q, kt, v, bias, w_out, w_ff1, w_ff2)
        x = _out_ffn_call(l, alpha, y_pool, y_attn, x, mod, wo_bf, ln1, w1_bf, w2_bf, ln2)
    return x
```

```python
import functools
import math

import jax
import jax.numpy as jnp
from jax import lax
from jax.experimental import pallas as pl
from jax.experimental.pallas import tpu as pltpu

D_MODEL = 1024
CHUNK = 64
D_POOL = D_MODEL // 2
POOL_WINDOWS = (2, 4, 8, 16)
POOL_GROUP_DIM = D_POOL // len(POOL_WINDOWS)
D_ATTN = D_MODEL - D_POOL
N_HEADS = 8
HEAD_DIM = D_ATTN // N_HEADS
LEFT_CHUNKS = 8
BAND = LEFT_CHUNKS + 1
REL_MAX = 128
REL_MIN = CHUNK - 1
N_REL = REL_MIN + REL_MAX + 1
D_FF = 4 * D_MODEL
D_IN = D_POOL + 3 * D_ATTN
N_MOD = 6
LN_EPS = 1e-5
MASK_VALUE = -1e30
LOG2E = math.log2(math.e)

LANES = 128
BAND_LEN = BAND * CHUNK
BAND_PAD = BAND_LEN + CHUNK
HEADS_PER_GROUP = LANES // HEAD_DIM
N_HEAD_GROUPS = N_HEADS // HEADS_PER_GROUP
GROUP_ROWS = HEADS_PER_GROUP * CHUNK
MAX_WINDOW = max(POOL_WINDOWS)

SEQ_TILE = LEFT_CHUNKS * CHUNK
CHUNKS_PER_TILE = SEQ_TILE // CHUNK
ROW_TILE = 512
FFN_ROW_TILE = 256
FFN_CHAINS = 4
IN_ROWS = 1024
ADA_COLS = 1536
FF_COLS = 1024

VMEM_LIMIT = 56 * 1024 * 1024

f32 = jnp.float32
bf16 = jnp.bfloat16


def _layer_block(l, shape):
    return pl.BlockSpec((1, *shape), lambda *_: (l,) + (0,) * len(shape),
                        pipeline_mode=pl.Buffered(1))


def _resident(shape):
    return pl.BlockSpec(shape, lambda *_: (0,) * len(shape), pipeline_mode=pl.Buffered(1))


def _params():
    return pltpu.CompilerParams(dimension_semantics=("arbitrary", "arbitrary"),
                                vmem_limit_bytes=VMEM_LIMIT)


def _layer_norm(z, g, b):
    mu = jnp.mean(z, axis=-1, keepdims=True)
    zc = z - mu
    var = jnp.mean(zc * zc, axis=-1, keepdims=True)
    return zc * lax.rsqrt(var + LN_EPS) * g + b


def _ada_kernel(c_ref, w_ref, b_ref, o_ref):
    c = c_ref[...]
    c_act = c / (1.0 + jnp.exp(-c))
    o_ref[0] = jnp.dot(c_act, w_ref[0], preferred_element_type=f32) + b_ref[0]


def _ada_call(c, w_ada, b_ada):
    depth, d, n = w_ada.shape
    batch = c.shape[0]
    return pl.pallas_call(
        _ada_kernel,
        grid=(depth, n // ADA_COLS),
        in_specs=[
            pl.BlockSpec((batch, d), lambda l, j: (0, 0)),
            pl.BlockSpec((1, d, ADA_COLS), lambda l, j: (l, 0, j)),
            pl.BlockSpec((1, 1, ADA_COLS), lambda l, j: (l, 0, j)),
        ],
        out_specs=pl.BlockSpec((1, batch, ADA_COLS), lambda l, j: (l, 0, j)),
        out_shape=jax.ShapeDtypeStruct((depth, batch, n), f32),
        compiler_params=_params(),
        name="ada_mod",
    )(c, w_ada, b_ada.reshape(depth, 1, n))


def _pool_windows(first_frame, u, halo):
    t_head = first_frame + lax.broadcasted_iota(jnp.int32, (MAX_WINDOW, POOL_GROUP_DIM), 0)
    pooled_groups = []
    for g, w in enumerate(POOL_WINDOWS):
        cols = slice(g * POOL_GROUP_DIM, (g + 1) * POOL_GROUP_DIM)
        tok = u[:, cols]
        win = jnp.concatenate([halo[:, cols], tok], axis=0)
        span = 1
        while span < w:
            win = win + pltpu.roll(win, span, axis=0)
            span *= 2
        win = win[MAX_WINDOW:]
        head = win[:MAX_WINDOW] / jnp.minimum(t_head + 1, w).astype(f32) - tok[:MAX_WINDOW]
        rest = win[MAX_WINDOW:] * (1.0 / w) - tok[MAX_WINDOW:]
        pooled_groups.append(jnp.concatenate([head, rest], axis=0).astype(bf16))
    return pooled_groups


def _pool_project(pooled_groups, wp_ref, ps_ref, y_ref, rows):
    for k in range(len(POOL_WINDOWS) // 2):
        cols = slice(2 * k * POOL_GROUP_DIM, 2 * (k + 1) * POOL_GROUP_DIM)
        pooled = jnp.concatenate(pooled_groups[2 * k:2 * k + 2], axis=1)
        yp = jnp.dot(pooled, wp_ref[0, k], preferred_element_type=f32)
        y_ref[0, rows, cols] = (yp * ps_ref[0, :, cols]).astype(bf16)


def _in_proj_kernel(x_ref, mod_ref, w_ref, wp_ref, ps_ref, wo_ref, w1_ref, w2_ref,
                    yp_ref, q_ref, kt_ref, v_ref, wo_bf_ref, w1_bf_ref, w2_bf_ref,
                    halo_scr, w_scr, wkt_scr):
    i = pl.program_id(1)
    k_lo, v_lo = D_POOL + D_ATTN, D_POOL + 2 * D_ATTN

    @pl.when((pl.program_id(0) == 0) & (i == 0))
    def _():
        for lo in range(0, k_lo, D_ATTN):
            w_scr[:, lo:lo + D_ATTN] = w_ref[0, :, lo:lo + D_ATTN].astype(bf16)
        w_scr[:, k_lo:v_lo] = w_ref[0, :, v_lo:D_IN].astype(bf16)
        wkt_scr[...] = w_ref[0, :, k_lo:v_lo].T.astype(bf16)

    @pl.when(i == 0)
    def _():
        halo_scr[...] = jnp.zeros((MAX_WINDOW, D_POOL), f32)

    for src, dst in ((wo_ref, wo_bf_ref), (w1_ref, w1_bf_ref), (w2_ref, w2_bf_ref)):
        dst[...] = src[0].astype(bf16)

    sh1 = mod_ref[0, 0, 0:1, :]
    sc1 = mod_ref[0, 0, 1:2, :]
    halo = halo_scr[...]
    pending = None
    for r in range(0, IN_ROWS, ROW_TILE):
        rows = slice(r, r + ROW_TILE)
        h = (x_ref[0, rows, :] * (1.0 + sc1) + sh1).astype(bf16)
        u = jnp.dot(h, w_scr[:, 0:D_POOL], preferred_element_type=f32)
        q = jnp.dot(h, w_scr[:, D_POOL:k_lo], preferred_element_type=f32)
        q_ref[0, rows, :] = (q * (HEAD_DIM ** -0.5 * LOG2E)).astype(bf16)
        if pending is not None:
            _pool_project(*pending)
        pending = (_pool_windows(i * IN_ROWS + r, u, halo), wp_ref, ps_ref, yp_ref, rows)
        halo = u[ROW_TILE - MAX_WINDOW:]
        kt = lax.dot_general(wkt_scr[...], h, (((1,), (1,)), ((), ())), preferred_element_type=f32)
        kt_ref[0, :, rows] = kt.astype(bf16)
        v = jnp.dot(h, w_scr[:, k_lo:v_lo], preferred_element_type=f32)
        v_ref[0, rows, :] = v.astype(bf16)
    _pool_project(*pending)
    halo_scr[...] = halo


def _in_proj_call(l, x, mod, w_in, w_pool, pool_scale, w_out, w_ff1, w_ff2):
    batch, seq, d = x.shape
    steps = batch * (seq // IN_ROWS)
    step = lambda b, i: b * (seq // IN_ROWS) + i
    weights = (w_out, w_ff1, w_ff2)
    slab_rows = [w.shape[1] // steps for w in weights]
    return pl.pallas_call(
        _in_proj_kernel,
        grid=(batch, seq // IN_ROWS),
        in_specs=[
            pl.BlockSpec((1, IN_ROWS, d), lambda b, i: (b, i, 0)),
            pl.BlockSpec((1, 1, N_MOD, d), lambda b, i: (l, b, 0, 0)),
            _layer_block(l, (d, D_IN)),
            _layer_block(l, w_pool.shape[1:]),
            _layer_block(l, pool_scale.shape[1:]),
        ] + [pl.BlockSpec((1, r, w.shape[2]), lambda b, i: (l, step(b, i), 0))
             for w, r in zip(weights, slab_rows)],
        out_specs=[
            pl.BlockSpec((1, IN_ROWS, D_POOL), lambda b, i: (b, i, 0)),
            pl.BlockSpec((1, IN_ROWS, D_ATTN), lambda b, i: (b, i, 0)),
            pl.BlockSpec((1, D_ATTN, IN_ROWS), lambda b, i: (b, 0, i)),
            pl.BlockSpec((1, IN_ROWS, D_ATTN), lambda b, i: (b, i, 0)),
        ] + [pl.BlockSpec((r, w.shape[2]), lambda b, i: (step(b, i), 0))
             for w, r in zip(weights, slab_rows)],
        out_shape=[
            jax.ShapeDtypeStruct((batch, seq, D_POOL), bf16),
            jax.ShapeDtypeStruct((batch, seq, D_ATTN), bf16),
            jax.ShapeDtypeStruct((batch, D_ATTN, seq), bf16),
            jax.ShapeDtypeStruct((batch, seq, D_ATTN), bf16),
        ] + [jax.ShapeDtypeStruct(w.shape[1:], bf16) for w in weights],
        scratch_shapes=[pltpu.VMEM((MAX_WINDOW, D_POOL), f32),
                        pltpu.VMEM((d, D_IN - D_ATTN), bf16),
                        pltpu.VMEM((D_ATTN, d), bf16)],
        compiler_params=_params(),
        name="in_proj",
    )(x, mod, w_in, w_pool, pool_scale, *weights)


def _attn_tile(first_tile, q_ref, kt_ref, v_ref, pkt_ref, pv_ref, bias_ref, y_ref, s_scr, m_scr,
               d_scr, p_scr):
    lane = lax.broadcasted_iota(jnp.int32, (CHUNK, LANES), 1)
    first_head = lane < HEAD_DIM

    def extents(c):
        a0 = LANES * (c // 2)
        b1 = LANES * (c // 2 + 1)
        w_prev = 0 if first_tile else SEQ_TILE - a0
        return a0, b1, w_prev

    def scores(c):
        a0, b1, w_prev = extents(c)
        b_lo = SEQ_TILE - a0 - w_prev
        for p in range(N_HEAD_GROUPS):
            grp = slice(p * LANES, (p + 1) * LANES)
            qp = q_ref[0, c * CHUNK:(c + 1) * CHUNK, grp]
            zero = jnp.zeros_like(qp)
            qs = jnp.concatenate([jnp.where(first_head, qp, zero),
                                  jnp.where(first_head, zero, qp)], axis=0)
            s = jnp.dot(qs, kt_ref[0, grp, 0:b1], preferred_element_type=f32)
            if w_prev:
                s = jnp.concatenate(
                    [jnp.dot(qs, pkt_ref[0, grp, a0:SEQ_TILE], preferred_element_type=f32), s],
                    axis=1)
            s = s + bias_ref[0, c % 2, p, :, b_lo:b_lo + w_prev + b1]
            s_scr[c % 2, p, :, 0:w_prev + b1] = s
            m_scr[c % 2, p] = jnp.broadcast_to(jnp.max(s, axis=-1, keepdims=True),
                                               (GROUP_ROWS, LANES))

    def exponentiate(c):
        a0, b1, w_prev = extents(c)
        width = w_prev + b1
        for p in range(N_HEAD_GROUPS):
            m = m_scr[c % 2, p]
            e = jnp.concatenate(
                [jnp.exp2(s_scr[c % 2, p, :, lo:lo + LANES] - m) for lo in range(0, width, LANES)],
                axis=1)
            d_scr[c % 2, p] = jnp.broadcast_to(jnp.sum(e, axis=-1, keepdims=True),
                                               (GROUP_ROWS, LANES))
            p_scr[c % 2, p, :, 0:width] = e.astype(bf16)

    def attend(c):
        a0, b1, w_prev = extents(c)
        width = w_prev + b1
        for p in range(N_HEAD_GROUPS):
            grp = slice(p * LANES, (p + 1) * LANES)
            o = jnp.dot(p_scr[c % 2, p, :, w_prev:width], v_ref[0, 0:b1, grp],
                        preferred_element_type=f32)
            if w_prev:
                o = o + jnp.dot(p_scr[c % 2, p, :, 0:w_prev], pv_ref[0, a0:SEQ_TILE, grp],
                                preferred_element_type=f32)
            o = o / d_scr[c % 2, p]
            out = jnp.where(first_head, o[0:CHUNK], o[CHUNK:GROUP_ROWS])
            y_ref[0, c * CHUNK:(c + 1) * CHUNK, grp] = out.astype(bf16)

    for t in range(CHUNKS_PER_TILE + 2):
        if t < CHUNKS_PER_TILE:
            scores(t)
        if 1 <= t <= CHUNKS_PER_TILE:
            exponentiate(t - 1)
        if t >= 2:
            attend(t - 2)


def _mixer_kernel(q_ref, kt_ref, v_ref, pkt_ref, pv_ref, bias_ref, y_ref, s_scr, m_scr, d_scr, p_scr):
    j = pl.program_id(1)
    attn = functools.partial(_attn_tile, q_ref=q_ref, kt_ref=kt_ref, v_ref=v_ref, pkt_ref=pkt_ref,
                             pv_ref=pv_ref, bias_ref=bias_ref, y_ref=y_ref, s_scr=s_scr,
                             m_scr=m_scr, d_scr=d_scr, p_scr=p_scr)
    pl.when(j == 0)(functools.partial(attn, True))
    pl.when(j > 0)(functools.partial(attn, False))


def _mixer_call(l, q, kt, v, bias):
    batch, seq, _ = q.shape
    rows = lambda back: pl.BlockSpec((1, SEQ_TILE, D_ATTN),
                                     lambda b, j: (b, jnp.maximum(j - back, 0), 0))
    cols = lambda back: pl.BlockSpec((1, D_ATTN, SEQ_TILE),
                                     lambda b, j: (b, 0, jnp.maximum(j - back, 0)))
    return pl.pallas_call(
        _mixer_kernel,
        grid=(batch, seq // SEQ_TILE),
        in_specs=[rows(0), cols(0), rows(0), cols(1), rows(1), _layer_block(l, bias.shape[1:])],
        out_specs=pl.BlockSpec((1, SEQ_TILE, D_ATTN), lambda b, j: (b, j, 0)),
        out_shape=jax.ShapeDtypeStruct((batch, seq, D_ATTN), bf16),
        scratch_shapes=[pltpu.VMEM((2, N_HEAD_GROUPS, GROUP_ROWS, BAND_PAD), f32),
                        pltpu.VMEM((2, N_HEAD_GROUPS, GROUP_ROWS, LANES), f32),
                        pltpu.VMEM((2, N_HEAD_GROUPS, GROUP_ROWS, LANES), f32),
                        pltpu.VMEM((2, N_HEAD_GROUPS, GROUP_ROWS, BAND_PAD), bf16)],
        compiler_params=_params(),
        name="mixer",
    )(q, kt, v, kt, v, bias)


def _out_ffn_kernel(alpha, yp_ref, ya_ref, x_ref, mod_ref, wo_ref, ln1_ref, w1_ref, w2_ref, ln2_ref,
                    o_ref):
    g1 = mod_ref[0, 0, 2:3, :]
    sh2 = mod_ref[0, 0, 3:4, :]
    sc2 = mod_ref[0, 0, 4:5, :]
    g2 = mod_ref[0, 0, 5:6, :]
    chains = [slice(i * FFN_ROW_TILE, (i + 1) * FFN_ROW_TILE) for i in range(FFN_CHAINS)]
    a = [jnp.dot(yp_ref[0, rows, :], wo_ref[0:D_POOL, :], preferred_element_type=f32)
         + jnp.dot(ya_ref[0, rows, :], wo_ref[D_POOL:D_MODEL, :], preferred_element_type=f32)
         for rows in chains]
    x1 = [_layer_norm(alpha * x_ref[0, rows, :] + (1.0 + g1) * a_i,
                      ln1_ref[0, 0:1, :], ln1_ref[0, 1:2, :]) for rows, a_i in zip(chains, a)]
    h = [(x1_i * (1.0 + sc2) + sh2).astype(bf16) for x1_i in x1]
    acc = [None] * FFN_CHAINS
    for s in range(D_FF // FF_COLS):
        cols = slice(s * FF_COLS, (s + 1) * FF_COLS)
        f = [jnp.dot(h_i, w1_ref[:, cols], preferred_element_type=f32) for h_i in h]
        f = [jnp.square(jnp.maximum(f_i, 0.0)).astype(bf16) for f_i in f]
        for i in range(FFN_CHAINS):
            part = jnp.dot(f[i], w2_ref[cols, :], preferred_element_type=f32)
            acc[i] = part if acc[i] is None else acc[i] + part
    for i, rows in enumerate(chains):
        o_ref[0, rows, :] = _layer_norm(alpha * x1[i] + (1.0 + g2) * acc[i],
                                        ln2_ref[0, 0:1, :], ln2_ref[0, 1:2, :])


def _out_ffn_call(l, alpha, y_pool, y_attn, x, mod, w_out, ln1, w_ff1, w_ff2, ln2):
    batch, seq, d = x.shape
    tile = lambda b, i: (b, i, 0)
    rows = FFN_CHAINS * FFN_ROW_TILE
    return pl.pallas_call(
        functools.partial(_out_ffn_kernel, alpha),
        grid=(batch, seq // rows),
        in_specs=[
            pl.BlockSpec((1, rows, D_POOL), tile),
            pl.BlockSpec((1, rows, D_ATTN), tile),
            pl.BlockSpec((1, rows, d), tile),
            pl.BlockSpec((1, 1, N_MOD, d), lambda b, i: (l, b, 0, 0)),
            _resident((d, d)),
            _layer_block(l, (2, d)),
            _resident((d, D_FF)),
            _resident((D_FF, d)),
            _layer_block(l, (2, d)),
        ],
        out_specs=pl.BlockSpec((1, rows, d), tile),
        out_shape=jax.ShapeDtypeStruct((batch, seq, d), f32),
        compiler_params=_params(),
        name="out_ffn",
    )(y_pool, y_attn, x, mod, w_out, ln1, w_ff1, w_ff2, ln2)


def _band_bias_tables(rel_bias):
    tab = rel_bias.astype(f32) * LOG2E
    lead = tab.shape[:-1]
    far = tab[..., -1:]
    n_near = REL_MAX + CHUNK
    period = n_near + CHUNK
    ring = jnp.concatenate([tab[..., ::-1], jnp.broadcast_to(far, (*lead, period - N_REL))], -1)
    near = jnp.tile(ring, (1, 1, CHUNK))[..., :CHUNK * (period - 1)]
    near = near.reshape(*lead, CHUNK, period - 1)[..., :n_near]
    clipped = jnp.broadcast_to(far[..., None], (*lead, CHUNK, BAND_LEN - n_near))
    band = jnp.concatenate([clipped, near], axis=-1)
    masked = jnp.full((*lead, CHUNK, CHUNK), MASK_VALUE, f32)
    tables = jnp.stack([jnp.concatenate([band, masked], -1),
                        jnp.concatenate([masked, band], -1)], axis=1)
    return tables.reshape(lead[0], 2, N_HEAD_GROUPS, GROUP_ROWS, BAND_PAD)


def _pair_block_diagonal(w_pool):
    depth, groups, n, _ = w_pool.shape
    w = w_pool.reshape(depth, groups // 2, 2, n, n)
    zero = jnp.zeros_like(w[:, :, 0])
    top = jnp.concatenate([w[:, :, 0], zero], axis=-1)
    bottom = jnp.concatenate([zero, w[:, :, 1]], axis=-1)
    return jnp.concatenate([top, bottom], axis=-2)


def kernel(x, c, w_ada, b_ada, w_in, w_pool, pool_scale, rel_bias, w_out, ln1_g, ln1_b,
           w_ff1, w_ff2, ln2_g, ln2_b):
    depth = w_in.shape[0]
    batch = x.shape[0]
    alpha = (2.0 * depth) ** 0.25
    mod = _ada_call(c, w_ada, b_ada).reshape(depth, batch, N_MOD, D_MODEL)
    bias = _band_bias_tables(rel_bias)
    w_pool = _pair_block_diagonal(w_pool.astype(bf16))
    pool_scale = pool_scale.reshape(depth, 1, D_POOL)
    ln1 = jnp.stack([ln1_g, ln1_b], axis=1)
    ln2 = jnp.stack([ln2_g, ln2_b], axis=1)
    for l in range(depth):
        y_pool, q, kt, v, wo_bf, w1_bf, w2_bf = _in_proj_call(l, x, mod, w_in, w_pool, pool_scale,
                                                              w_out, w_ff1, w_ff2)
        y_attn = _mixer_call(l, q, kt, v, bias)
        x = _out_ffn_call(l, alpha, y_pool, y_attn, x, mod, wo_bf, ln1, w1_bf, w2_bf, ln2)
    return x
```

```python
import functools
import math

import jax
import jax.numpy as jnp
from jax import lax
from jax.experimental import pallas as pl
from jax.experimental.pallas import tpu as pltpu

D_MODEL = 1024
CHUNK = 64
D_POOL = D_MODEL // 2
POOL_WINDOWS = (2, 4, 8, 16)
POOL_GROUP_DIM = D_POOL // len(POOL_WINDOWS)
D_ATTN = D_MODEL - D_POOL
N_HEADS = 8
HEAD_DIM = D_ATTN // N_HEADS
LEFT_CHUNKS = 8
BAND = LEFT_CHUNKS + 1
REL_MAX = 128
REL_MIN = CHUNK - 1
N_REL = REL_MIN + REL_MAX + 1
D_FF = 4 * D_MODEL
D_IN = D_POOL + 3 * D_ATTN
N_MOD = 6
LN_EPS = 1e-5
MASK_VALUE = -1e30
LOG2E = math.log2(math.e)

LANES = 128
BAND_LEN = BAND * CHUNK
BAND_PAD = BAND_LEN + CHUNK
HEADS_PER_GROUP = LANES // HEAD_DIM
N_HEAD_GROUPS = N_HEADS // HEADS_PER_GROUP
GROUP_ROWS = HEADS_PER_GROUP * CHUNK
MAX_WINDOW = max(POOL_WINDOWS)

SEQ_TILE = LEFT_CHUNKS * CHUNK
CHUNKS_PER_TILE = SEQ_TILE // CHUNK
ROW_TILE = 512
FFN_ROW_TILE = 256
FFN_CHAINS = 4
IN_ROWS = 1024
ADA_ROWS = 256
FF_COLS = 1024

VMEM_LIMIT = 56 * 1024 * 1024

f32 = jnp.float32
bf16 = jnp.bfloat16


def _layer_block(l, shape):
    return pl.BlockSpec((1, *shape), lambda *_: (l,) + (0,) * len(shape),
                        pipeline_mode=pl.Buffered(1))


def _resident(shape):
    return pl.BlockSpec(shape, lambda *_: (0,) * len(shape), pipeline_mode=pl.Buffered(1))


def _params():
    return pltpu.CompilerParams(dimension_semantics=("arbitrary", "arbitrary"),
                                vmem_limit_bytes=VMEM_LIMIT)


def _layer_norm(z, g, b):
    mu = jnp.mean(z, axis=-1, keepdims=True)
    zc = z - mu
    var = jnp.mean(zc * zc, axis=-1, keepdims=True)
    return zc * lax.rsqrt(var + LN_EPS) * g + b


def _ada_kernel(c_ref, w_ref, b_ref, o_ref):
    k = pl.program_id(1)
    c = c_ref[0]
    c_act = c / (1.0 + jnp.exp(-c))
    part = jnp.dot(c_act, w_ref[0], preferred_element_type=f32)

    @pl.when(k == 0)
    def _():
        o_ref[0] = part + b_ref[0]

    @pl.when(k > 0)
    def _():
        o_ref[0] += part


def _ada_call(c, w_ada, b_ada):
    depth, d, n = w_ada.shape
    batch = c.shape[0]
    c_slabs = c.reshape(batch, d // ADA_ROWS, ADA_ROWS).swapaxes(0, 1)
    return pl.pallas_call(
        _ada_kernel,
        grid=(depth, d // ADA_ROWS),
        in_specs=[
            pl.BlockSpec((1, batch, ADA_ROWS), lambda l, k: (k, 0, 0)),
            pl.BlockSpec((1, ADA_ROWS, n), lambda l, k: (l, k, 0)),
            pl.BlockSpec((1, 1, n), lambda l, k: (l, 0, 0)),
        ],
        out_specs=pl.BlockSpec((1, batch, n), lambda l, k: (l, 0, 0)),
        out_shape=jax.ShapeDtypeStruct((depth, batch, n), f32),
        compiler_params=_params(),
        name="ada_mod",
    )(c_slabs, w_ada, b_ada.reshape(depth, 1, n))


def _pool_windows(first_frame, u, halo):
    t_head = first_frame + lax.broadcasted_iota(jnp.int32, (MAX_WINDOW, POOL_GROUP_DIM), 0)
    pooled_groups = []
    for g, w in enumerate(POOL_WINDOWS):
        cols = slice(g * POOL_GROUP_DIM, (g + 1) * POOL_GROUP_DIM)
        tok = u[:, cols]
        win = jnp.concatenate([halo[:, cols], tok], axis=0)
        span = 1
        while span < w:
            win = win + pltpu.roll(win, span, axis=0)
            span *= 2
        win = win[MAX_WINDOW:]
        head = win[:MAX_WINDOW] / jnp.minimum(t_head + 1, w).astype(f32) - tok[:MAX_WINDOW]
        rest = win[MAX_WINDOW:] * (1.0 / w) - tok[MAX_WINDOW:]
        pooled_groups.append(jnp.concatenate([head, rest], axis=0).astype(bf16))
    return pooled_groups


def _pool_project(pooled_groups, wp_ref, ps_ref, y_ref, rows):
    for k in range(len(POOL_WINDOWS) // 2):
        cols = slice(2 * k * POOL_GROUP_DIM, 2 * (k + 1) * POOL_GROUP_DIM)
        pooled = jnp.concatenate(pooled_groups[2 * k:2 * k + 2], axis=1)
        yp = jnp.dot(pooled, wp_ref[0, k], preferred_element_type=f32)
        y_ref[0, rows, cols] = (yp * ps_ref[0, :, cols]).astype(bf16)


def _in_proj_kernel(x_ref, mod_ref, w_ref, wp_ref, ps_ref, wo_ref, w1_ref, w2_ref,
                    yp_ref, q_ref, kt_ref, v_ref, wo_bf_ref, w1_bf_ref, w2_bf_ref,
                    halo_scr, w_scr, wkt_scr):
    i = pl.program_id(1)
    k_lo, v_lo = D_POOL + D_ATTN, D_POOL + 2 * D_ATTN

    @pl.when((pl.program_id(0) == 0) & (i == 0))
    def _():
        for lo in range(0, k_lo, D_ATTN):
            w_scr[:, lo:lo + D_ATTN] = w_ref[0, :, lo:lo + D_ATTN].astype(bf16)
        w_scr[:, k_lo:v_lo] = w_ref[0, :, v_lo:D_IN].astype(bf16)
        wkt_scr[...] = w_ref[0, :, k_lo:v_lo].T.astype(bf16)

    @pl.when(i == 0)
    def _():
        halo_scr[...] = jnp.zeros((MAX_WINDOW, D_POOL), f32)

    for src, dst in ((wo_ref, wo_bf_ref), (w1_ref, w1_bf_ref), (w2_ref, w2_bf_ref)):
        dst[...] = src[0].astype(bf16)

    sh1 = mod_ref[0, 0, 0:1, :]
    sc1 = mod_ref[0, 0, 1:2, :]
    halo = halo_scr[...]
    pending = None
    for r in range(0, IN_ROWS, ROW_TILE):
        rows = slice(r, r + ROW_TILE)
        h = (x_ref[0, rows, :] * (1.0 + sc1) + sh1).astype(bf16)
        u = jnp.dot(h, w_scr[:, 0:D_POOL], preferred_element_type=f32)
        q = jnp.dot(h, w_scr[:, D_POOL:k_lo], preferred_element_type=f32)
        q_ref[0, rows, :] = (q * (HEAD_DIM ** -0.5 * LOG2E)).astype(bf16)
        if pending is not None:
            _pool_project(*pending)
        pending = (_pool_windows(i * IN_ROWS + r, u, halo), wp_ref, ps_ref, yp_ref, rows)
        halo = u[ROW_TILE - MAX_WINDOW:]
        kt = lax.dot_general(wkt_scr[...], h, (((1,), (1,)), ((), ())), preferred_element_type=f32)
        kt_ref[0, :, rows] = kt.astype(bf16)
        v = jnp.dot(h, w_scr[:, k_lo:v_lo], preferred_element_type=f32)
        v_ref[0, rows, :] = v.astype(bf16)
    _pool_project(*pending)
    halo_scr[...] = halo


def _in_proj_call(l, x, mod, w_in, w_pool, pool_scale, w_out, w_ff1, w_ff2):
    batch, seq, d = x.shape
    steps = batch * (seq // IN_ROWS)
    step = lambda b, i: b * (seq // IN_ROWS) + i
    weights = (w_out, w_ff1, w_ff2)
    slab_rows = [w.shape[1] // steps for w in weights]
    return pl.pallas_call(
        _in_proj_kernel,
        grid=(batch, seq // IN_ROWS),
        in_specs=[
            pl.BlockSpec((1, IN_ROWS, d), lambda b, i: (b, i, 0)),
            pl.BlockSpec((1, 1, N_MOD, d), lambda b, i: (l, b, 0, 0)),
            _layer_block(l, (d, D_IN)),
            _layer_block(l, w_pool.shape[1:]),
            _layer_block(l, pool_scale.shape[1:]),
        ] + [pl.BlockSpec((1, r, w.shape[2]), lambda b, i: (l, step(b, i), 0))
             for w, r in zip(weights, slab_rows)],
        out_specs=[
            pl.BlockSpec((1, IN_ROWS, D_POOL), lambda b, i: (b, i, 0)),
            pl.BlockSpec((1, IN_ROWS, D_ATTN), lambda b, i: (b, i, 0)),
            pl.BlockSpec((1, D_ATTN, IN_ROWS), lambda b, i: (b, 0, i)),
            pl.BlockSpec((1, IN_ROWS, D_ATTN), lambda b, i: (b, i, 0)),
        ] + [pl.BlockSpec((r, w.shape[2]), lambda b, i: (step(b, i), 0))
             for w, r in zip(weights, slab_rows)],
        out_shape=[
            jax.ShapeDtypeStruct((batch, seq, D_POOL), bf16),
            jax.ShapeDtypeStruct((batch, seq, D_ATTN), bf16),
            jax.ShapeDtypeStruct((batch, D_ATTN, seq), bf16),
            jax.ShapeDtypeStruct((batch, seq, D_ATTN), bf16),
        ] + [jax.ShapeDtypeStruct(w.shape[1:], bf16) for w in weights],
        scratch_shapes=[pltpu.VMEM((MAX_WINDOW, D_POOL), f32),
                        pltpu.VMEM((d, D_IN - D_ATTN), bf16),
                        pltpu.VMEM((D_ATTN, d), bf16)],
        compiler_params=_params(),
        name="in_proj",
    )(x, mod, w_in, w_pool, pool_scale, *weights)


def _attn_tile(first_tile, q_ref, kt_ref, v_ref, pkt_ref, pv_ref, bias_ref, y_ref, s_scr, m_scr,
               d_scr, p_scr):
    lane = lax.broadcasted_iota(jnp.int32, (CHUNK, LANES), 1)
    first_head = lane < HEAD_DIM

    def extents(c):
        a0 = LANES * (c // 2)
        b1 = LANES * (c // 2 + 1)
        w_prev = 0 if first_tile else SEQ_TILE - a0
        return a0, b1, w_prev

    def scores(c):
        a0, b1, w_prev = extents(c)
        b_lo = SEQ_TILE - a0 - w_prev
        for p in range(N_HEAD_GROUPS):
            grp = slice(p * LANES, (p + 1) * LANES)
            qp = q_ref[0, c * CHUNK:(c + 1) * CHUNK, grp]
            zero = jnp.zeros_like(qp)
            qs = jnp.concatenate([jnp.where(first_head, qp, zero),
                                  jnp.where(first_head, zero, qp)], axis=0)
            s = jnp.dot(qs, kt_ref[0, grp, 0:b1], preferred_element_type=f32)
            if w_prev:
                s = jnp.concatenate(
                    [jnp.dot(qs, pkt_ref[0, grp, a0:SEQ_TILE], preferred_element_type=f32), s],
                    axis=1)
            s = s + bias_ref[0, c % 2, p, :, b_lo:b_lo + w_prev + b1]
            s_scr[c % 2, p, :, 0:w_prev + b1] = s
            m_scr[c % 2, p] = jnp.broadcast_to(jnp.max(s, axis=-1, keepdims=True),
                                               (GROUP_ROWS, LANES))

    def exponentiate(c):
        a0, b1, w_prev = extents(c)
        width = w_prev + b1
        for p in range(N_HEAD_GROUPS):
            m = m_scr[c % 2, p]
            e = jnp.concatenate(
                [jnp.exp2(s_scr[c % 2, p, :, lo:lo + LANES] - m) for lo in range(0, width, LANES)],
                axis=1)
            d_scr[c % 2, p] = jnp.broadcast_to(jnp.sum(e, axis=-1, keepdims=True),
                                               (GROUP_ROWS, LANES))
            p_scr[c % 2, p, :, 0:width] = e.astype(bf16)

    def attend(c):
        a0, b1, w_prev = extents(c)
        width = w_prev + b1
        for p in range(N_HEAD_GROUPS):
            grp = slice(p * LANES, (p + 1) * LANES)
            o = jnp.dot(p_scr[c % 2, p, :, w_prev:width], v_ref[0, 0:b1, grp],
                        preferred_element_type=f32)
            if w_prev:
                o = o + jnp.dot(p_scr[c % 2, p, :, 0:w_prev], pv_ref[0, a0:SEQ_TILE, grp],
                                preferred_element_type=f32)
            o = o / d_scr[c % 2, p]
            out = jnp.where(first_head, o[0:CHUNK], o[CHUNK:GROUP_ROWS])
            y_ref[0, c * CHUNK:(c + 1) * CHUNK, grp] = out.astype(bf16)

    for t in range(CHUNKS_PER_TILE + 2):
        if t < CHUNKS_PER_TILE:
            scores(t)
        if 1 <= t <= CHUNKS_PER_TILE:
            exponentiate(t - 1)
        if t >= 2:
            attend(t - 2)


def _mixer_kernel(q_ref, kt_ref, v_ref, pkt_ref, pv_ref, bias_ref, y_ref, s_scr, m_scr, d_scr, p_scr):
    j = pl.program_id(1)
    attn = functools.partial(_attn_tile, q_ref=q_ref, kt_ref=kt_ref, v_ref=v_ref, pkt_ref=pkt_ref,
                             pv_ref=pv_ref, bias_ref=bias_ref, y_ref=y_ref, s_scr=s_scr,
                             m_scr=m_scr, d_scr=d_scr, p_scr=p_scr)
    pl.when(j == 0)(functools.partial(attn, True))
    pl.when(j > 0)(functools.partial(attn, False))


def _mixer_call(l, q, kt, v, bias):
    batch, seq, _ = q.shape
    rows = lambda back: pl.BlockSpec((1, SEQ_TILE, D_ATTN),
                                     lambda b, j: (b, jnp.maximum(j - back, 0), 0))
    cols = lambda back: pl.BlockSpec((1, D_ATTN, SEQ_TILE),
                                     lambda b, j: (b, 0, jnp.maximum(j - back, 0)))
    return pl.pallas_call(
        _mixer_kernel,
        grid=(batch, seq // SEQ_TILE),
        in_specs=[rows(0), cols(0), rows(0), cols(1), rows(1), _layer_block(l, bias.shape[1:])],
        out_specs=pl.BlockSpec((1, SEQ_TILE, D_ATTN), lambda b, j: (b, j, 0)),
        out_shape=jax.ShapeDtypeStruct((batch, seq, D_ATTN), bf16),
        scratch_shapes=[pltpu.VMEM((2, N_HEAD_GROUPS, GROUP_ROWS, BAND_PAD), f32),
                        pltpu.VMEM((2, N_HEAD_GROUPS, GROUP_ROWS, LANES), f32),
                        pltpu.VMEM((2, N_HEAD_GROUPS, GROUP_ROWS, LANES), f32),
                        pltpu.VMEM((2, N_HEAD_GROUPS, GROUP_ROWS, BAND_PAD), bf16)],
        compiler_params=_params(),
        name="mixer",
    )(q, kt, v, kt, v, bias)


def _out_ffn_kernel(alpha, yp_ref, ya_ref, x_ref, mod_ref, wo_ref, ln1_ref, w1_ref, w2_ref, ln2_ref,
                    o_ref):
    g1 = mod_ref[0, 0, 2:3, :]
    sh2 = mod_ref[0, 0, 3:4, :]
    sc2 = mod_ref[0, 0, 4:5, :]
    g2 = mod_ref[0, 0, 5:6, :]
    chains = [slice(i * FFN_ROW_TILE, (i + 1) * FFN_ROW_TILE) for i in range(FFN_CHAINS)]
    a = [jnp.dot(yp_ref[0, rows, :], wo_ref[0:D_POOL, :], preferred_element_type=f32)
         + jnp.dot(ya_ref[0, rows, :], wo_ref[D_POOL:D_MODEL, :], preferred_element_type=f32)
         for rows in chains]
    x1 = [_layer_norm(alpha * x_ref[0, rows, :] + (1.0 + g1) * a_i,
                      ln1_ref[0, 0:1, :], ln1_ref[0, 1:2, :]) for rows, a_i in zip(chains, a)]
    h = [(x1_i * (1.0 + sc2) + sh2).astype(bf16) for x1_i in x1]
    acc = [None] * FFN_CHAINS
    for s in range(D_FF // FF_COLS):
        cols = slice(s * FF_COLS, (s + 1) * FF_COLS)
        f = [jnp.dot(h_i, w1_ref[:, cols], preferred_element_type=f32) for h_i in h]
        f = [jnp.square(jnp.maximum(f_i, 0.0)).astype(bf16) for f_i in f]
        for i in range(FFN_CHAINS):
            part = jnp.dot(f[i], w2_ref[cols, :], preferred_element_type=f32)
            acc[i] = part if acc[i] is None else acc[i] + part
    for i, rows in enumerate(chains):
        o_ref[0, rows, :] = _layer_norm(alpha * x1[i] + (1.0 + g2) * acc[i],
                                        ln2_ref[0, 0:1, :], ln2_ref[0, 1:2, :])


def _out_ffn_call(l, alpha, y_pool, y_attn, x, mod, w_out, ln1, w_ff1, w_ff2, ln2):
    batch, seq, d = x.shape
    tile = lambda b, i: (b, i, 0)
    rows = FFN_CHAINS * FFN_ROW_TILE
    return pl.pallas_call(
        functools.partial(_out_ffn_kernel, alpha),
        grid=(batch, seq // rows),
        in_specs=[
            pl.BlockSpec((1, rows, D_POOL), tile),
            pl.BlockSpec((1, rows, D_ATTN), tile),
            pl.BlockSpec((1, rows, d), tile),
            pl.BlockSpec((1, 1, N_MOD, d), lambda b, i: (l, b, 0, 0)),
            _resident((d, d)),
            _layer_block(l, (2, d)),
            _resident((d, D_FF)),
            _resident((D_FF, d)),
            _layer_block(l, (2, d)),
        ],
        out_specs=pl.BlockSpec((1, rows, d), tile),
        out_shape=jax.ShapeDtypeStruct((batch, seq, d), f32),
        compiler_params=_params(),
        name="out_ffn",
    )(y_pool, y_attn, x, mod, w_out, ln1, w_ff1, w_ff2, ln2)


def _band_bias_tables(rel_bias):
    tab = rel_bias.astype(f32) * LOG2E
    lead = tab.shape[:-1]
    far = tab[..., -1:]
    n_near = REL_MAX + CHUNK
    period = n_near + CHUNK
    ring = jnp.concatenate([tab[..., ::-1], jnp.broadcast_to(far, (*lead, period - N_REL))], -1)
    near = jnp.tile(ring, (1, 1, CHUNK))[..., :CHUNK * (period - 1)]
    near = near.reshape(*lead, CHUNK, period - 1)[..., :n_near]
    clipped = jnp.broadcast_to(far[..., None], (*lead, CHUNK, BAND_LEN - n_near))
    band = jnp.concatenate([clipped, near], axis=-1)
    masked = jnp.full((*lead, CHUNK, CHUNK), MASK_VALUE, f32)
    tables = jnp.stack([jnp.concatenate([band, masked], -1),
                        jnp.concatenate([masked, band], -1)], axis=1)
    return tables.reshape(lead[0], 2, N_HEAD_GROUPS, GROUP_ROWS, BAND_PAD)


def _pair_block_diagonal(w_pool):
    depth, groups, n, _ = w_pool.shape
    w = w_pool.reshape(depth, groups // 2, 2, n, n)
    zero = jnp.zeros_like(w[:, :, 0])
    top = jnp.concatenate([w[:, :, 0], zero], axis=-1)
    bottom = jnp.concatenate([zero, w[:, :, 1]], axis=-1)
    return jnp.concatenate([top, bottom], axis=-2)


def kernel(x, c, w_ada, b_ada, w_in, w_pool, pool_scale, rel_bias, w_out, ln1_g, ln1_b,
           w_ff1, w_ff2, ln2_g, ln2_b):
    depth = w_in.shape[0]
    batch = x.shape[0]
    alpha = (2.0 * depth) ** 0.25
    mod = _ada_call(c, w_ada, b_ada).reshape(depth, batch, N_MOD, D_MODEL)
    bias = _band_bias_tables(rel_bias)
    w_pool = _pair_block_diagonal(w_pool.astype(bf16))
    pool_scale = pool_scale.reshape(depth, 1, D_POOL)
    ln1 = jnp.stack([ln1_g, ln1_b], axis=1)
    ln2 = jnp.stack([ln2_g, ln2_b], axis=1)
    for l in range(depth):
        y_pool, q, kt, v, wo_bf, w1_bf, w2_bf = _in_proj_call(l, x, mod, w_in, w_pool, pool_scale,
                                                              w_out, w_ff1, w_ff2)
        y_attn = _mixer_call(l, q, kt, v, bias)
        x = _out_ffn_call(l, alpha, y_pool, y_attn, x, mod, wo_bf, ln1, w1_bf, w2_bf, ln2)
    return x
```

```python
import functools
import math

import jax
import jax.numpy as jnp
from jax import lax
from jax.experimental import pallas as pl
from jax.experimental.pallas import tpu as pltpu

D_MODEL = 1024
CHUNK = 64
D_POOL = D_MODEL // 2
POOL_WINDOWS = (2, 4, 8, 16)
POOL_GROUP_DIM = D_POOL // len(POOL_WINDOWS)
D_ATTN = D_MODEL - D_POOL
N_HEADS = 8
HEAD_DIM = D_ATTN // N_HEADS
LEFT_CHUNKS = 8
BAND = LEFT_CHUNKS + 1
REL_MAX = 128
REL_MIN = CHUNK - 1
N_REL = REL_MIN + REL_MAX + 1
D_FF = 4 * D_MODEL
D_IN = D_POOL + 3 * D_ATTN
N_MOD = 6
LN_EPS = 1e-5
MASK_VALUE = -1e30
LOG2E = math.log2(math.e)

LANES = 128
BAND_LEN = BAND * CHUNK
BAND_PAD = BAND_LEN + CHUNK
HEADS_PER_GROUP = LANES // HEAD_DIM
N_HEAD_GROUPS = N_HEADS // HEADS_PER_GROUP
GROUP_ROWS = HEADS_PER_GROUP * CHUNK
MAX_WINDOW = max(POOL_WINDOWS)

SEQ_TILE = LEFT_CHUNKS * CHUNK
CHUNKS_PER_TILE = SEQ_TILE // CHUNK
ROW_TILE = 512
FFN_ROW_TILE = 256
FFN_CHAINS = 4
IN_ROWS = 1024
ADA_ROWS = 256
FF_COLS = 1024

VMEM_LIMIT = 56 * 1024 * 1024

f32 = jnp.float32
bf16 = jnp.bfloat16


def _layer_block(l, shape):
    return pl.BlockSpec((1, *shape), lambda *_: (l,) + (0,) * len(shape),
                        pipeline_mode=pl.Buffered(1))


def _resident(shape):
    return pl.BlockSpec(shape, lambda *_: (0,) * len(shape), pipeline_mode=pl.Buffered(1))


def _params(fuse_inputs=None):
    return pltpu.CompilerParams(dimension_semantics=("arbitrary", "arbitrary"),
                                vmem_limit_bytes=VMEM_LIMIT, allow_input_fusion=fuse_inputs)


def _layer_norm(z, g, b):
    mu = jnp.mean(z, axis=-1, keepdims=True)
    zc = z - mu
    var = jnp.mean(zc * zc, axis=-1, keepdims=True)
    return zc * lax.rsqrt(var + LN_EPS) * g + b


def _ada_kernel(c_ref, w_ref, b_ref, o_ref):
    k = pl.program_id(1)
    c = c_ref[0]
    c_act = c / (1.0 + jnp.exp(-c))
    part = jnp.dot(c_act, w_ref[0], preferred_element_type=f32)

    @pl.when(k == 0)
    def _():
        o_ref[0] = part + b_ref[0]

    @pl.when(k > 0)
    def _():
        o_ref[0] += part


def _ada_call(c, w_ada, b_ada):
    depth, d, n = w_ada.shape
    batch = c.shape[0]
    c_slabs = c.reshape(batch, d // ADA_ROWS, ADA_ROWS).swapaxes(0, 1)
    return pl.pallas_call(
        _ada_kernel,
        grid=(depth, d // ADA_ROWS),
        in_specs=[
            pl.BlockSpec((1, batch, ADA_ROWS), lambda l, k: (k, 0, 0)),
            pl.BlockSpec((1, ADA_ROWS, n), lambda l, k: (l, k, 0)),
            pl.BlockSpec((1, 1, n), lambda l, k: (l, 0, 0)),
        ],
        out_specs=pl.BlockSpec((1, batch, n), lambda l, k: (l, 0, 0)),
        out_shape=jax.ShapeDtypeStruct((depth, batch, n), f32),
        compiler_params=_params(),
        name="ada_mod",
    )(c_slabs, w_ada, b_ada.reshape(depth, 1, n))


def _pool_windows(first_frame, u, halo):
    t_head = first_frame + lax.broadcasted_iota(jnp.int32, (MAX_WINDOW, POOL_GROUP_DIM), 0)
    pooled_groups = []
    for g, w in enumerate(POOL_WINDOWS):
        cols = slice(g * POOL_GROUP_DIM, (g + 1) * POOL_GROUP_DIM)
        tok = u[:, cols]
        win = jnp.concatenate([halo[:, cols], tok], axis=0)
        span = 1
        while span < w:
            win = win + pltpu.roll(win, span, axis=0)
            span *= 2
        win = win[MAX_WINDOW:]
        head = win[:MAX_WINDOW] / jnp.minimum(t_head + 1, w).astype(f32) - tok[:MAX_WINDOW]
        rest = win[MAX_WINDOW:] * (1.0 / w) - tok[MAX_WINDOW:]
        pooled_groups.append(jnp.concatenate([head, rest], axis=0).astype(bf16))
    return pooled_groups


def _pool_project(pooled_groups, wp_ref, ps_ref, y_ref, rows):
    for k in range(len(POOL_WINDOWS) // 2):
        cols = slice(2 * k * POOL_GROUP_DIM, 2 * (k + 1) * POOL_GROUP_DIM)
        pooled = jnp.concatenate(pooled_groups[2 * k:2 * k + 2], axis=1)
        yp = jnp.dot(pooled, wp_ref[0, k], preferred_element_type=f32)
        y_ref[0, rows, cols] = (yp * ps_ref[0, :, cols]).astype(bf16)


def _in_proj_kernel(x_ref, mod_ref, w_ref, wp_ref, ps_ref, wo_ref, w1_ref, w2_ref,
                    yp_ref, q_ref, kt_ref, v_ref, wo_bf_ref, w1_bf_ref, w2_bf_ref,
                    halo_scr, w_scr, wkt_scr):
    i = pl.program_id(1)
    k_lo, v_lo = D_POOL + D_ATTN, D_POOL + 2 * D_ATTN

    @pl.when((pl.program_id(0) == 0) & (i == 0))
    def _():
        for lo in range(0, k_lo, D_ATTN):
            w_scr[:, lo:lo + D_ATTN] = w_ref[0, :, lo:lo + D_ATTN].astype(bf16)
        w_scr[:, k_lo:v_lo] = w_ref[0, :, v_lo:D_IN].astype(bf16)
        wkt_scr[...] = w_ref[0, :, k_lo:v_lo].T.astype(bf16)

    @pl.when(i == 0)
    def _():
        halo_scr[...] = jnp.zeros((MAX_WINDOW, D_POOL), f32)

    for src, dst in ((wo_ref, wo_bf_ref), (w1_ref, w1_bf_ref), (w2_ref, w2_bf_ref)):
        dst[...] = src[0].astype(bf16)

    sh1 = mod_ref[0, 0, 0:1, :]
    sc1 = mod_ref[0, 0, 1:2, :]
    halo = halo_scr[...]
    pending = None
    for r in range(0, IN_ROWS, ROW_TILE):
        rows = slice(r, r + ROW_TILE)
        h = (x_ref[0, rows, :] * (1.0 + sc1) + sh1).astype(bf16)
        u = jnp.dot(h, w_scr[:, 0:D_POOL], preferred_element_type=f32)
        q = jnp.dot(h, w_scr[:, D_POOL:k_lo], preferred_element_type=f32)
        q_ref[0, rows, :] = (q * (HEAD_DIM ** -0.5 * LOG2E)).astype(bf16)
        if pending is not None:
            _pool_project(*pending)
        pending = (_pool_windows(i * IN_ROWS + r, u, halo), wp_ref, ps_ref, yp_ref, rows)
        halo = u[ROW_TILE - MAX_WINDOW:]
        kt = lax.dot_general(wkt_scr[...], h, (((1,), (1,)), ((), ())), preferred_element_type=f32)
        kt_ref[0, :, rows] = kt.astype(bf16)
        v = jnp.dot(h, w_scr[:, k_lo:v_lo], preferred_element_type=f32)
        v_ref[0, rows, :] = v.astype(bf16)
    _pool_project(*pending)
    halo_scr[...] = halo


def _in_proj_call(l, x, mod, w_in, w_pool, pool_scale, w_out, w_ff1, w_ff2):
    batch, seq, d = x.shape
    steps = batch * (seq // IN_ROWS)
    step = lambda b, i: b * (seq // IN_ROWS) + i
    weights = (w_out, w_ff1, w_ff2)
    slab_rows = [w.shape[1] // steps for w in weights]
    return pl.pallas_call(
        _in_proj_kernel,
        grid=(batch, seq // IN_ROWS),
        in_specs=[
            pl.BlockSpec((1, IN_ROWS, d), lambda b, i: (b, i, 0)),
            pl.BlockSpec((1, 1, N_MOD, d), lambda b, i: (l, b, 0, 0)),
            _layer_block(l, (d, D_IN)),
            _layer_block(l, w_pool.shape[1:]),
            _layer_block(l, pool_scale.shape[1:]),
        ] + [pl.BlockSpec((1, r, w.shape[2]), lambda b, i: (l, step(b, i), 0))
             for w, r in zip(weights, slab_rows)],
        out_specs=[
            pl.BlockSpec((1, IN_ROWS, D_POOL), lambda b, i: (b, i, 0)),
            pl.BlockSpec((1, IN_ROWS, D_ATTN), lambda b, i: (b, i, 0)),
            pl.BlockSpec((1, D_ATTN, IN_ROWS), lambda b, i: (b, 0, i)),
            pl.BlockSpec((1, IN_ROWS, D_ATTN), lambda b, i: (b, i, 0)),
        ] + [pl.BlockSpec((r, w.shape[2]), lambda b, i: (step(b, i), 0))
             for w, r in zip(weights, slab_rows)],
        out_shape=[
            jax.ShapeDtypeStruct((batch, seq, D_POOL), bf16),
            jax.ShapeDtypeStruct((batch, seq, D_ATTN), bf16),
            jax.ShapeDtypeStruct((batch, D_ATTN, seq), bf16),
            jax.ShapeDtypeStruct((batch, seq, D_ATTN), bf16),
        ] + [jax.ShapeDtypeStruct(w.shape[1:], bf16) for w in weights],
        scratch_shapes=[pltpu.VMEM((MAX_WINDOW, D_POOL), f32),
                        pltpu.VMEM((d, D_IN - D_ATTN), bf16),
                        pltpu.VMEM((D_ATTN, d), bf16)],
        compiler_params=_params([False, True, False, True, True, False, False, False]),
        name="in_proj",
    )(x, mod, w_in, w_pool, pool_scale, *weights)


def _attn_tile(first_tile, q_ref, kt_ref, v_ref, pkt_ref, pv_ref, bias_ref, y_ref, s_scr, m_scr,
               d_scr, p_scr):
    lane = lax.broadcasted_iota(jnp.int32, (CHUNK, LANES), 1)
    first_head = lane < HEAD_DIM

    def extents(c):
        a0 = LANES * (c // 2)
        b1 = LANES * (c // 2 + 1)
        w_prev = 0 if first_tile else SEQ_TILE - a0
        return a0, b1, w_prev

    def scores(c):
        a0, b1, w_prev = extents(c)
        b_lo = SEQ_TILE - a0 - w_prev
        for p in range(N_HEAD_GROUPS):
            grp = slice(p * LANES, (p + 1) * LANES)
            qp = q_ref[0, c * CHUNK:(c + 1) * CHUNK, grp]
            zero = jnp.zeros_like(qp)
            qs = jnp.concatenate([jnp.where(first_head, qp, zero),
                                  jnp.where(first_head, zero, qp)], axis=0)
            s = jnp.dot(qs, kt_ref[0, grp, 0:b1], preferred_element_type=f32)
            if w_prev:
                s = jnp.concatenate(
                    [jnp.dot(qs, pkt_ref[0, grp, a0:SEQ_TILE], preferred_element_type=f32), s],
                    axis=1)
            s = s + bias_ref[0, c % 2, p, :, b_lo:b_lo + w_prev + b1]
            s_scr[c % 2, p, :, 0:w_prev + b1] = s
            m_scr[c % 2, p] = jnp.broadcast_to(jnp.max(s, axis=-1, keepdims=True),
                                               (GROUP_ROWS, LANES))

    def exponentiate(c):
        a0, b1, w_prev = extents(c)
        width = w_prev + b1
        for p in range(N_HEAD_GROUPS):
            m = m_scr[c % 2, p]
            e = jnp.concatenate(
                [jnp.exp2(s_scr[c % 2, p, :, lo:lo + LANES] - m) for lo in range(0, width, LANES)],
                axis=1)
            d_scr[c % 2, p] = jnp.broadcast_to(jnp.sum(e, axis=-1, keepdims=True),
                                               (GROUP_ROWS, LANES))
            p_scr[c % 2, p, :, 0:width] = e.astype(bf16)

    def attend(c):
        a0, b1, w_prev = extents(c)
        width = w_prev + b1
        for p in range(N_HEAD_GROUPS):
            grp = slice(p * LANES, (p + 1) * LANES)
            o = jnp.dot(p_scr[c % 2, p, :, w_prev:width], v_ref[0, 0:b1, grp],
                        preferred_element_type=f32)
            if w_prev:
                o = o + jnp.dot(p_scr[c % 2, p, :, 0:w_prev], pv_ref[0, a0:SEQ_TILE, grp],
                                preferred_element_type=f32)
            o = o / d_scr[c % 2, p]
            out = jnp.where(first_head, o[0:CHUNK], o[CHUNK:GROUP_ROWS])
            y_ref[0, c * CHUNK:(c + 1) * CHUNK, grp] = out.astype(bf16)

    for t in range(CHUNKS_PER_TILE + 2):
        if t < CHUNKS_PER_TILE:
            scores(t)
        if 1 <= t <= CHUNKS_PER_TILE:
            exponentiate(t - 1)
        if t >= 2:
            attend(t - 2)


def _mixer_kernel(q_ref, kt_ref, v_ref, pkt_ref, pv_ref, bias_ref, y_ref, s_scr, m_scr, d_scr, p_scr):
    j = pl.program_id(1)
    attn = functools.partial(_attn_tile, q_ref=q_ref, kt_ref=kt_ref, v_ref=v_ref, pkt_ref=pkt_ref,
                             pv_ref=pv_ref, bias_ref=bias_ref, y_ref=y_ref, s_scr=s_scr,
                             m_scr=m_scr, d_scr=d_scr, p_scr=p_scr)
    pl.when(j == 0)(functools.partial(attn, True))
    pl.when(j > 0)(functools.partial(attn, False))


def _mixer_call(l, q, kt, v, bias):
    batch, seq, _ = q.shape
    rows = lambda back: pl.BlockSpec((1, SEQ_TILE, D_ATTN),
                                     lambda b, j: (b, jnp.maximum(j - back, 0), 0))
    cols = lambda back: pl.BlockSpec((1, D_ATTN, SEQ_TILE),
                                     lambda b, j: (b, 0, jnp.maximum(j - back, 0)))
    return pl.pallas_call(
        _mixer_kernel,
        grid=(batch, seq // SEQ_TILE),
        in_specs=[rows(0), cols(0), rows(0), cols(1), rows(1), _layer_block(l, bias.shape[1:])],
        out_specs=pl.BlockSpec((1, SEQ_TILE, D_ATTN), lambda b, j: (b, j, 0)),
        out_shape=jax.ShapeDtypeStruct((batch, seq, D_ATTN), bf16),
        scratch_shapes=[pltpu.VMEM((2, N_HEAD_GROUPS, GROUP_ROWS, BAND_PAD), f32),
                        pltpu.VMEM((2, N_HEAD_GROUPS, GROUP_ROWS, LANES), f32),
                        pltpu.VMEM((2, N_HEAD_GROUPS, GROUP_ROWS, LANES), f32),
                        pltpu.VMEM((2, N_HEAD_GROUPS, GROUP_ROWS, BAND_PAD), bf16)],
        compiler_params=_params([False, False, False, False, False, True]),
        name="mixer",
    )(q, kt, v, kt, v, bias)


def _out_ffn_kernel(alpha, yp_ref, ya_ref, x_ref, mod_ref, wo_ref, ln1_ref, w1_ref, w2_ref, ln2_ref,
                    o_ref):
    g1 = mod_ref[0, 0, 2:3, :]
    sh2 = mod_ref[0, 0, 3:4, :]
    sc2 = mod_ref[0, 0, 4:5, :]
    g2 = mod_ref[0, 0, 5:6, :]
    chains = [slice(i * FFN_ROW_TILE, (i + 1) * FFN_ROW_TILE) for i in range(FFN_CHAINS)]
    a = [jnp.dot(yp_ref[0, rows, :], wo_ref[0:D_POOL, :], preferred_element_type=f32)
         + jnp.dot(ya_ref[0, rows, :], wo_ref[D_POOL:D_MODEL, :], preferred_element_type=f32)
         for rows in chains]
    x1 = [_layer_norm(alpha * x_ref[0, rows, :] + (1.0 + g1) * a_i,
                      ln1_ref[0, 0:1, :], ln1_ref[0, 1:2, :]) for rows, a_i in zip(chains, a)]
    h = [(x1_i * (1.0 + sc2) + sh2).astype(bf16) for x1_i in x1]
    acc = [None] * FFN_CHAINS
    for s in range(D_FF // FF_COLS):
        cols = slice(s * FF_COLS, (s + 1) * FF_COLS)
        f = [jnp.dot(h_i, w1_ref[:, cols], preferred_element_type=f32) for h_i in h]
        f = [jnp.square(jnp.maximum(f_i, 0.0)).astype(bf16) for f_i in f]
        for i in range(FFN_CHAINS):
            part = jnp.dot(f[i], w2_ref[cols, :], preferred_element_type=f32)
            acc[i] = part if acc[i] is None else acc[i] + part
    for i, rows in enumerate(chains):
        o_ref[0, rows, :] = _layer_norm(alpha * x1[i] + (1.0 + g2) * acc[i],
                                        ln2_ref[0, 0:1, :], ln2_ref[0, 1:2, :])


def _out_ffn_call(l, alpha, y_pool, y_attn, x, mod, w_out, ln1, w_ff1, w_ff2, ln2):
    batch, seq, d = x.shape
    tile = lambda b, i: (b, i, 0)
    rows = FFN_CHAINS * FFN_ROW_TILE
    return pl.pallas_call(
        functools.partial(_out_ffn_kernel, alpha),
        grid=(batch, seq // rows),
        in_specs=[
            pl.BlockSpec((1, rows, D_POOL), tile),
            pl.BlockSpec((1, rows, D_ATTN), tile),
            pl.BlockSpec((1, rows, d), tile),
            pl.BlockSpec((1, 1, N_MOD, d), lambda b, i: (l, b, 0, 0)),
            _resident((d, d)),
            _layer_block(l, (2, d)),
            _resident((d, D_FF)),
            _resident((D_FF, d)),
            _layer_block(l, (2, d)),
        ],
        out_specs=pl.BlockSpec((1, rows, d), tile),
        out_shape=jax.ShapeDtypeStruct((batch, seq, d), f32),
        compiler_params=_params([False, False, False, True, False, True, False, False, True]),
        name="out_ffn",
    )(y_pool, y_attn, x, mod, w_out, ln1, w_ff1, w_ff2, ln2)


def _band_bias_tables(rel_bias):
    tab = rel_bias.astype(f32) * LOG2E
    lead = tab.shape[:-1]
    far = tab[..., -1:]
    n_near = REL_MAX + CHUNK
    period = n_near + CHUNK
    ring = jnp.concatenate([tab[..., ::-1], jnp.broadcast_to(far, (*lead, period - N_REL))], -1)
    near = jnp.tile(ring, (1, 1, CHUNK))[..., :CHUNK * (period - 1)]
    near = near.reshape(*lead, CHUNK, period - 1)[..., :n_near]
    clipped = jnp.broadcast_to(far[..., None], (*lead, CHUNK, BAND_LEN - n_near))
    band = jnp.concatenate([clipped, near], axis=-1)
    masked = jnp.full((*lead, CHUNK, CHUNK), MASK_VALUE, f32)
    tables = jnp.stack([jnp.concatenate([band, masked], -1),
                        jnp.concatenate([masked, band], -1)], axis=1)
    return tables.reshape(lead[0], 2, N_HEAD_GROUPS, GROUP_ROWS, BAND_PAD)


def _pair_block_diagonal(w_pool):
    depth, groups, n, _ = w_pool.shape
    w = w_pool.reshape(depth, groups // 2, 2, n, n)
    zero = jnp.zeros_like(w[:, :, 0])
    top = jnp.concatenate([w[:, :, 0], zero], axis=-1)
    bottom = jnp.concatenate([zero, w[:, :, 1]], axis=-1)
    return jnp.concatenate([top, bottom], axis=-2)


def kernel(x, c, w_ada, b_ada, w_in, w_pool, pool_scale, rel_bias, w_out, ln1_g, ln1_b,
           w_ff1, w_ff2, ln2_g, ln2_b):
    depth = w_in.shape[0]
    batch = x.shape[0]
    alpha = (2.0 * depth) ** 0.25
    mod = _ada_call(c, w_ada, b_ada).reshape(depth, batch, N_MOD, D_MODEL)
    bias = _band_bias_tables(rel_bias)
    w_pool = _pair_block_diagonal(w_pool.astype(bf16))
    pool_scale = pool_scale.reshape(depth, 1, D_POOL)
    ln1 = jnp.stack([ln1_g, ln1_b], axis=1)
    ln2 = jnp.stack([ln2_g, ln2_b], axis=1)
    for l in range(depth):
        y_pool, q, kt, v, wo_bf, w1_bf, w2_bf = _in_proj_call(l, x, mod, w_in, w_pool, pool_scale,
                                                              w_out, w_ff1, w_ff2)
        y_attn = _mixer_call(l, q, kt, v, bias)
        x = _out_ffn_call(l, alpha, y_pool, y_attn, x, mod, wo_bf, ln1, w1_bf, w2_bf, ln2)
    return x
```
